```python
import math
import jax, jax.numpy as jnp
from jax import lax
import numpy as np


D_MODEL = 1024
BATCH = 8
SEQ = 4096
DEPTH = 1
DEC_BATCH = 128
DEC_SEQ = 8
PAST_LEN = 8192
PAGE_SIZE = 128

ATT_HEADS = 4
ATT_DH = 64
ATT_DV = 2 * ATT_DH
D_ATT = ATT_HEADS * ATT_DV
D_CONV = D_MODEL - D_ATT
CONV_W = 31
D_IN = 3 * D_ATT + 2 * D_CONV
ROPE_THETA = 10000.0
Q_BLOCK = 128
MEM_TOKENS = 256
MEM_HEADS = 4
MEM_DH = D_MODEL // MEM_HEADS
PEER_HEADS = 8
N_KEYS = 128
N_EXPERTS = N_KEYS * N_KEYS
PEER_DK = 256
PEER_TOPK = 16
PEER_BLOCK = 256
EPS = 1e-6

kernel_name = 'hymba_diffattn_conformer_peer_step'


def _rmsnorm(x, g):
    xf = x.astype(jnp.float32)
    y = xf * lax.rsqrt(jnp.mean(xf * xf, axis=-1, keepdims=True) + EPS)
    return (y * g.astype(jnp.float32)).astype(x.dtype)


def _layernorm(x, g, b):
    xf = x.astype(jnp.float32)
    mu = jnp.mean(xf, axis=-1, keepdims=True)
    var = jnp.mean(jnp.square(xf - mu), axis=-1, keepdims=True)
    y = (xf - mu) * lax.rsqrt(var + EPS) * g.astype(jnp.float32) + b.astype(jnp.float32)
    return y.astype(x.dtype)


def _rope(x, pos):
    half = ATT_DH // 2
    inv = ROPE_THETA ** (-jnp.arange(0, ATT_DH, 2, dtype=jnp.float32) / ATT_DH)
    ang = pos.astype(jnp.float32)[:, None] * inv[None, :]
    cos = jnp.cos(ang)[:, None, None, :]
    sin = jnp.sin(ang)[:, None, None, :]
    xf = x.astype(jnp.float32)
    x1, x2 = xf[..., :half], xf[..., half:]
    return jnp.concatenate([x1 * cos - x2 * sin, x2 * cos + x1 * sin], axis=-1).astype(x.dtype)


def _mixer_inputs(xn, w_in, pos):
    B, T, _ = xn.shape
    z = xn @ w_in
    q, k, v, a, gate = jnp.split(z, [D_ATT, 2 * D_ATT, 3 * D_ATT, 3 * D_ATT + D_CONV], axis=-1)
    q = _rope(q.reshape(B, T, ATT_HEADS, 2, ATT_DH), pos)
    k = _rope(k.reshape(B, T, ATT_HEADS, 2, ATT_DH), pos)
    v = v.reshape(B, T, ATT_HEADS, ATT_DV)
    u = a * jax.nn.sigmoid(gate)
    return q, k, v, u


def _diff_attn_prompt(q, k, v, lam):
    B, S = q.shape[:2]
    nb = S // Q_BLOCK
    scale = ATT_DH ** -0.5
    qb = q.reshape(B, nb, Q_BLOCK, ATT_HEADS, 2, ATT_DH).transpose(1, 0, 2, 3, 4, 5)
    kpos = jnp.arange(S)

    def block(args):
        qi, i = args
        s = jnp.einsum('bqhcd,bkhcd->bhcqk', qi, k).astype(jnp.float32) * scale
        qpos = i * Q_BLOCK + jnp.arange(Q_BLOCK)
        s = jnp.where(kpos[None, :] <= qpos[:, None], s, -jnp.inf)
        p = jax.nn.softmax(s, axis=-1)
        pd = p[:, :, 0] - lam * p[:, :, 1]
        return jnp.einsum('bhqk,bkhv->bqhv', pd.astype(v.dtype), v)

    out = lax.map(block, (qb, jnp.arange(nb)))
    return out.transpose(1, 0, 2, 3, 4).reshape(B, S, ATT_HEADS, ATT_DV)


def _online(carry, s, v):
    m, l, acc = carry
    m_new = jnp.maximum(m, jnp.max(s, axis=-1))
    alpha = jnp.exp(m - m_new)
    p = jnp.exp(s - m_new[..., None])
    l = l * alpha + jnp.sum(p, axis=-1)
    acc = acc * alpha[..., None] + jnp.einsum('bhcqk,bkhv->bhcqv', p, v.astype(jnp.float32))
    return (m_new, l, acc)


def _diff_attn_sample(q, k_new, v_new, cache_k, cache_v, li, page_table, lam):
    Bd, T = q.shape[:2]
    scale = ATT_DH ** -0.5
    init = (jnp.full((Bd, ATT_HEADS, 2, T), -jnp.inf, jnp.float32),
            jnp.zeros((Bd, ATT_HEADS, 2, T), jnp.float32),
            jnp.zeros((Bd, ATT_HEADS, 2, T, ATT_DV), jnp.float32))

    def page_step(carry, pages):
        kp = cache_k[li, pages].reshape(Bd, PAGE_SIZE, ATT_HEADS, 2, ATT_DH)
        vp = cache_v[li, pages]
        s = jnp.einsum('bqhcd,bkhcd->bhcqk', q, kp.astype(q.dtype)).astype(jnp.float32) * scale
        return _online(carry, s, vp), None

    carry, _ = lax.scan(page_step, init, page_table.T)
    s = jnp.einsum('bqhcd,bkhcd->bhcqk', q, k_new).astype(jnp.float32) * scale
    s = jnp.where(jnp.tril(jnp.ones((T, T), bool)), s, -jnp.inf)
    m, l, acc = _online(carry, s, v_new)
    o = acc / l[..., None]
    od = o[:, :, 0] - lam * o[:, :, 1]
    return od.transpose(0, 2, 1, 3).astype(v_new.dtype)


def _dwconv(u_pad, w, b):
    out = lax.conv_general_dilated(u_pad, w[:, None, :].astype(u_pad.dtype), window_strides=(1,),
                                   padding='VALID', dimension_numbers=('NWC', 'WIO', 'NWC'),
                                   feature_group_count=D_CONV)
    return out + b


def _mixer_out(att, u_pad, lam_init, g_subln, conv_w, conv_b, conv_ln_g, conv_ln_b, w_out):
    B, T = att.shape[:2]
    a = (_rmsnorm(att, g_subln) * (1.0 - lam_init)).reshape(B, T, D_ATT)
    c = jax.nn.silu(_layernorm(_dwconv(u_pad, conv_w, conv_b), conv_ln_g, conv_ln_b))
    return jnp.concatenate([a, c], axis=-1) @ w_out


def _mem_kv(mem, g_src, w_mk, w_mv):
    B = mem.shape[0]
    mn = _rmsnorm(mem, g_src)
    k = (mn @ w_mk).reshape(B, MEM_TOKENS, MEM_HEADS, MEM_DH)
    v = (mn @ w_mv).reshape(B, MEM_TOKENS, MEM_HEADS, MEM_DH)
    return k, v


def _mem_attn(hn, mk, mv, w_mq, w_mo):
    B, T, _ = hn.shape
    q = (hn @ w_mq).reshape(B, T, MEM_HEADS, MEM_DH)
    s = jnp.einsum('bqhd,bkhd->bhqk', q, mk.astype(q.dtype)).astype(jnp.float32) * (MEM_DH ** -0.5)
    p = jax.nn.softmax(s, axis=-1)
    o = jnp.einsum('bhqk,bkhd->bqhd', p.astype(hn.dtype), mv.astype(hn.dtype)).reshape(B, T, D_MODEL)
    return o @ w_mo


def _peer(xn, w_pq, sub_keys, u_tab, v_tab):
    B, T, D = xn.shape
    n = B * T
    nblk = -(-n // PEER_BLOCK)
    xt = jnp.pad(xn.reshape(n, D), ((0, nblk * PEER_BLOCK - n), (0, 0))).reshape(nblk, PEER_BLOCK, D)

    def block(xb):
        q = (xb @ w_pq).reshape(PEER_BLOCK, PEER_HEADS, 2, PEER_DK // 2)
        s = jnp.einsum('thcd,hcnd->thcn', q, sub_keys).astype(jnp.float32)
        sv, si = lax.top_k(s, PEER_TOPK)
        cand = (sv[..., 0, :, None] + sv[..., 1, None, :]).reshape(PEER_BLOCK, PEER_HEADS, PEER_TOPK * PEER_TOPK)
        cidx = (si[..., 0, :, None] * N_KEYS + si[..., 1, None, :]).reshape(PEER_BLOCK, PEER_HEADS, PEER_TOPK * PEER_TOPK)
        tv, tp = lax.top_k(cand, PEER_TOPK)
        eidx = jnp.take_along_axis(cidx, tp, axis=-1)
        g = jax.nn.softmax(tv, axis=-1)
        ue = u_tab[eidx]
        ve = v_tab[eidx]
        hact = jax.nn.gelu(jnp.einsum('thkd,td->thk', ue, xb), approximate=False)
        return jnp.einsum('thk,thkd->td', (g * hact).astype(ve.dtype), ve)

    out = lax.map(block, xt).reshape(nblk * PEER_BLOCK, D)[:n]
    return out.reshape(B, T, D)


def setup_inputs(seed: int = 0) -> dict:
    key = jax.random.key(seed)
    ks = jax.random.split(key, 40)
    n_pages = PAST_LEN // PAGE_SIZE
    n_pool = (DEC_BATCH * n_pages * 5) // 4
    f32 = jnp.float32
    L = DEPTH

    def nrm(k, shape, scale):
        return jax.random.normal(k, shape, f32) * scale

    def gain(k, shape):
        return 1.0 + 0.02 * jax.random.normal(k, shape, f32)

    page_table = jax.random.permutation(ks[0], n_pool)[: DEC_BATCH * n_pages]
    page_table = page_table.reshape(DEC_BATCH, n_pages).astype(jnp.int32)
    return {
        'x_prompt': nrm(ks[1], (BATCH, SEQ, D_MODEL), 1.0),
        'x_sample': nrm(ks[2], (DEC_BATCH, DEC_SEQ, D_MODEL), 1.0),
        'cache_k': nrm(ks[3], (L, n_pool, PAGE_SIZE, ATT_HEADS, 2 * ATT_DH), 1.0),
        'cache_v': nrm(ks[4], (L, n_pool, PAGE_SIZE, ATT_HEADS, ATT_DV), 1.0),
        'state_conv': nrm(ks[5], (L, DEC_BATCH, CONV_W - 1, D_CONV), 1.0),
        'cache_mem_k': nrm(ks[6], (L, DEC_BATCH, MEM_TOKENS, MEM_HEADS, MEM_DH), 1.0),
        'cache_mem_v': nrm(ks[7], (L, DEC_BATCH, MEM_TOKENS, MEM_HEADS, MEM_DH), 1.0),
        'page_table': page_table,
        'mem_prompt': nrm(ks[8], (BATCH, MEM_TOKENS, D_MODEL), 1.0),
        'g_norm_mix': gain(ks[9], (L, D_MODEL)),
        'w_in': nrm(ks[10], (L, D_MODEL, D_IN), D_MODEL ** -0.5),
        'lam_q1': nrm(ks[11], (L, ATT_DH), 0.1),
        'lam_k1': nrm(ks[12], (L, ATT_DH), 0.1),
        'lam_q2': nrm(ks[13], (L, ATT_DH), 0.1),
        'lam_k2': nrm(ks[14], (L, ATT_DH), 0.1),
        'g_subln': gain(ks[15], (L, ATT_DV)),
        'conv_w': nrm(ks[16], (L, CONV_W, D_CONV), CONV_W ** -0.5),
        'conv_b': nrm(ks[17], (L, D_CONV), 0.02),
        'conv_ln_g': gain(ks[18], (L, D_CONV)),
        'conv_ln_b': nrm(ks[19], (L, D_CONV), 0.02),
        'w_out': nrm(ks[20], (L, D_MODEL, D_MODEL), D_MODEL ** -0.5),
        'g_norm_mem': gain(ks[21], (L, D_MODEL)),
        'g_mem_src': gain(ks[22], (L, D_MODEL)),
        'w_mq': nrm(ks[23], (L, D_MODEL, D_MODEL), D_MODEL ** -0.5),
        'w_mk': nrm(ks[24], (L, D_MODEL, D_MODEL), D_MODEL ** -0.5),
        'w_mv': nrm(ks[25], (L, D_MODEL, D_MODEL), D_MODEL ** -0.5),
        'w_mo': nrm(ks[26], (L, D_MODEL, D_MODEL), D_MODEL ** -0.5),
        'g_norm_peer': gain(ks[27], (L, D_MODEL)),
        'w_pq': nrm(ks[28], (L, D_MODEL, PEER_HEADS * PEER_DK), D_MODEL ** -0.5),
        'sub_keys': nrm(ks[29], (L, PEER_HEADS, 2, N_KEYS, PEER_DK // 2), (PEER_DK // 2) ** -0.5),
        'u_tab': nrm(ks[30], (L, N_EXPERTS, D_MODEL), D_MODEL ** -0.5),
        'v_tab': nrm(ks[31], (L, N_EXPERTS, D_MODEL), PEER_HEADS ** -0.5),
        'g_final': gain(ks[32], (D_MODEL,)),
    }


def reference(x_prompt, x_sample, cache_k, cache_v, state_conv, cache_mem_k, cache_mem_v,
              page_table, mem_prompt, g_norm_mix, w_in, lam_q1, lam_k1, lam_q2, lam_k2,
              g_subln, conv_w, conv_b, conv_ln_g, conv_ln_b, w_out, g_norm_mem, g_mem_src,
              w_mq, w_mk, w_mv, w_mo, g_norm_peer, w_pq, sub_keys, u_tab, v_tab, g_final):
    xp, xs = x_prompt, x_sample
    pos_p = jnp.arange(SEQ, dtype=jnp.int32)
    pos_s = PAST_LEN + jnp.arange(DEC_SEQ, dtype=jnp.int32)
    kp_l, vp_l, cp_l, mkp_l, mvp_l, ks_l, vs_l, cs_l = [], [], [], [], [], [], [], []
    for li in range(DEPTH):
        lam_init = 0.8 - 0.6 * math.exp(-0.3 * li)
        lam = (jnp.exp(jnp.sum(lam_q1[li] * lam_k1[li]).astype(jnp.float32))
               - jnp.exp(jnp.sum(lam_q2[li] * lam_k2[li]).astype(jnp.float32)) + lam_init)
        mix_w = (lam_init, g_subln[li], conv_w[li], conv_b[li], conv_ln_g[li], conv_ln_b[li], w_out[li])

        q, k, v, u = _mixer_inputs(_rmsnorm(xp, g_norm_mix[li]), w_in[li], pos_p)
        att = _diff_attn_prompt(q, k, v, lam)
        u_pad = jnp.pad(u, ((0, 0), (CONV_W - 1, 0), (0, 0)))
        xp = xp + _mixer_out(att, u_pad, *mix_w)
        kp_l.append(k.reshape(BATCH, SEQ, ATT_HEADS, 2 * ATT_DH))
        vp_l.append(v)
        cp_l.append(u[:, SEQ - (CONV_W - 1):])

        q, k, v, u = _mixer_inputs(_rmsnorm(xs, g_norm_mix[li]), w_in[li], pos_s)
        att = _diff_attn_sample(q, k, v, cache_k, cache_v, li, page_table, lam)
        u_cat = jnp.concatenate([state_conv[li].astype(u.dtype), u], axis=1)
        xs = xs + _mixer_out(att, u_cat, *mix_w)
        ks_l.append(k.reshape(DEC_BATCH, DEC_SEQ, ATT_HEADS, 2 * ATT_DH))
        vs_l.append(v)
        cs_l.append(u_cat[:, -(CONV_W - 1):])

        mk, mv = _mem_kv(mem_prompt, g_mem_src[li], w_mk[li], w_mv[li])
        xp = xp + _mem_attn(_rmsnorm(xp, g_norm_mem[li]), mk, mv, w_mq[li], w_mo[li])
        mkp_l.append(mk)
        mvp_l.append(mv)
        xs = xs + _mem_attn(_rmsnorm(xs, g_norm_mem[li]), cache_mem_k[li], cache_mem_v[li], w_mq[li], w_mo[li])

        xp = xp + _peer(_rmsnorm(xp, g_norm_peer[li]), w_pq[li], sub_keys[li], u_tab[li], v_tab[li])
        xs = xs + _peer(_rmsnorm(xs, g_norm_peer[li]), w_pq[li], sub_keys[li], u_tab[li], v_tab[li])

    y_prompt = _rmsnorm(xp, g_final)
    y_sample = _rmsnorm(xs, g_final)
    return (y_prompt, y_sample, jnp.stack(kp_l), jnp.stack(vp_l), jnp.stack(cp_l),
            jnp.stack(mkp_l), jnp.stack(mvp_l), jnp.stack(ks_l), jnp.stack(vs_l), jnp.stack(cs_l))
```

```python
import functools
import math

import jax
import jax.numpy as jnp
from jax import lax
from jax.experimental import pallas as pl
from jax.experimental.pallas import tpu as pltpu

F32 = jnp.float32
BF16 = jnp.bfloat16
I32 = jnp.int32

EPS = 1e-6
ROPE_THETA = 10000.0
ATT_HEADS = 4
ATT_DH = 64
ATT_DV = 128
D_ATT = 512
D_CONV = 512
CONV_W = 31
CONV_HIST = 32
MEM_HEADS = 4
PEER_HEADS = 8
N_KEYS = 128
PEER_TOPK = 16
LANES = 128
VMEM_LIMIT = 56 * 1024 * 1024

_NT = (((1,), (1,)), ((), ()))


def _params(sem, vmem=VMEM_LIMIT):
    return pltpu.CompilerParams(dimension_semantics=sem, vmem_limit_bytes=vmem)


def _rmsnorm(x, g):
    return x * lax.rsqrt(jnp.mean(x * x, axis=-1, keepdims=True) + EPS) * g


def _gelu(x):
    return 0.5 * x * (1.0 + lax.erf(x * (2.0 ** -0.5)))


def _lam(lq1, lk1, lq2, lk2, lam_init):
    a = jnp.sum(lq1 * lk1, axis=-1, keepdims=True)
    b = jnp.sum(lq2 * lk2, axis=-1, keepdims=True)
    return jnp.exp(a) - jnp.exp(b) + lam_init


def _mixer_in_body(x_ref, g_ref, w_ref, cos_ref, sin_ref, q_ref, k_ref, v_ref, u_ref):
    xn = _rmsnorm(x_ref[...], g_ref[...])
    z = jnp.dot(xn.astype(BF16), w_ref[...], preferred_element_type=F32)
    cos = cos_ref[...]
    sin = sin_ref[...]
    lane = lax.broadcasted_iota(I32, (1, LANES), 1)
    first_half = (lane & 32) == 0

    def rope(t):
        partner = jnp.where(first_half, pltpu.roll(t, 96, 1), pltpu.roll(t, 32, 1))
        return t * cos + partner * sin

    for h in range(ATT_HEADS):
        lo = h * LANES
        q_ref[:, lo:lo + LANES] = rope(z[:, lo:lo + LANES])
        k_ref[:, lo:lo + LANES] = rope(z[:, D_ATT + lo:D_ATT + lo + LANES])
    v_ref[...] = z[:, 2 * D_ATT:3 * D_ATT]
    a = z[:, 3 * D_ATT:3 * D_ATT + D_CONV]
    gate = z[:, 3 * D_ATT + D_CONV:]
    u_ref[...] = a * jax.nn.sigmoid(gate)


def _rope_tables(pos):
    inv = ROPE_THETA ** (-jnp.arange(0, ATT_DH, 2, dtype=F32) / ATT_DH)
    ang = pos.astype(F32)[:, None] * inv[None, :]
    cos = jnp.tile(jnp.cos(ang), (1, 4))
    sin = jnp.sin(ang)
    sin = jnp.concatenate([-sin, sin, -sin, sin], axis=1)
    return cos, sin


def _mixer_in(x2d, g, w_bf16, cos, sin, *, tm, interpret=False):
    n, d = x2d.shape
    d_in = w_bf16.shape[1]
    nt = cos.shape[0] // tm
    tab = pl.BlockSpec((tm, LANES), lambda i: (i % nt, 0))
    out = pl.BlockSpec((tm, D_ATT), lambda i: (i, 0))
    return pl.pallas_call(
        _mixer_in_body,
        out_shape=[jax.ShapeDtypeStruct((n, D_ATT), F32)] * 4,
        grid=(n // tm,),
        in_specs=[pl.BlockSpec((tm, d), lambda i: (i, 0)),
                  pl.BlockSpec((1, d), lambda i: (0, 0)),
                  pl.BlockSpec((d, d_in), lambda i: (0, 0)),
                  tab, tab],
        out_specs=[out] * 4,
        compiler_params=_params(("parallel",)),
        name="mixer_in", interpret=interpret,
    )(x2d, g, w_bf16, cos, sin)


def _attn_prompt_body(lq1, lk1, lq2, lk2, q_ref, k_ref, v_ref, o_ref, *, tq, lam_init):
    qi = pl.program_id(2)
    lam = _lam(lq1[...], lk1[...], lq2[...], lk2[...], lam_init)
    q = q_ref[...] * (ATT_DH ** -0.5)
    lane = lax.broadcasted_iota(I32, (1, LANES), 1)
    qq = jnp.concatenate([jnp.where(lane < ATT_DH, q, 0.0),
                          jnp.where(lane >= ATT_DH, q, 0.0)], axis=0).astype(BF16)

    def step(j, carry, masked):
        m, l, acc = carry
        start = pl.multiple_of(j * tq, tq)
        kj = k_ref[pl.ds(start, tq), :].astype(BF16)
        vj = v_ref[pl.ds(start, tq), :].astype(BF16)
        s = lax.dot_general(qq, kj, _NT, preferred_element_type=F32)
        if masked:
            qpos = lax.broadcasted_iota(I32, s.shape, 0) & (tq - 1)
            kpos = lax.broadcasted_iota(I32, s.shape, 1)
            s = jnp.where(kpos <= qpos, s, -jnp.inf)
        m_new = jnp.maximum(m, jnp.max(s, axis=-1, keepdims=True))
        alpha = jnp.exp(m - m_new)
        p = jnp.exp(s - m_new)
        l = l * alpha + jnp.sum(p, axis=-1, keepdims=True)
        acc = acc * alpha + jnp.dot(p.astype(BF16), vj, preferred_element_type=F32)
        return m_new, l, acc

    init = (jnp.full((2 * tq, 1), -jnp.inf, F32), jnp.zeros((2 * tq, 1), F32),
            jnp.zeros((2 * tq, ATT_DV), F32))
    carry = lax.fori_loop(0, qi, lambda j, c: step(j, c, False), init)
    _, l, acc = step(qi, carry, True)
    o = acc / l
    o_ref[...] = o[:tq] - lam * o[tq:]


def _attn_prompt(lams, q, k, v, *, tq, lam_init, interpret=False):
    b, s, _ = q.shape
    assert tq & (tq - 1) == 0
    lam_spec = pl.BlockSpec((1, ATT_DH), lambda bi, h, i: (0, 0))
    kv_spec = pl.BlockSpec((None, s, LANES), lambda bi, h, i: (bi, 0, h))
    qo_spec = pl.BlockSpec((None, tq, LANES), lambda bi, h, i: (bi, i, h))
    return pl.pallas_call(
        functools.partial(_attn_prompt_body, tq=tq, lam_init=lam_init),
        out_shape=jax.ShapeDtypeStruct((b, s, D_ATT), F32),
        grid=(b, ATT_HEADS, s // tq),
        in_specs=[lam_spec] * 4 + [qo_spec, kv_spec, kv_spec],
        out_specs=qo_spec,
        compiler_params=_params(("parallel", "parallel", "arbitrary")),
        name="attn_prompt", interpret=interpret,
    )(*lams, q, k, v)


def _attn_sample_body(pt_ref, lq1, lk1, lq2, lk2, q_ref, kn_ref, vn_ref, *rest, pp, t, lam_init):
    k_refs = rest[:pp]
    v_refs = rest[pp:2 * pp]
    o_ref = rest[2 * pp]
    qbd_ref, m_ref, l_ref, acc_ref = rest[2 * pp + 1:]
    g = pl.program_id(1)
    rows = ATT_HEADS * 2 * t

    @pl.when(g == 0)
    def _():
        q = q_ref[...] * (ATT_DH ** -0.5)
        col = lax.broadcasted_iota(I32, q.shape, 1)
        blocks = []
        for h in range(ATT_HEADS):
            for c in range(2):
                lo = h * LANES + c * ATT_DH
                blocks.append(jnp.where((col >= lo) & (col < lo + ATT_DH), q, 0.0))
        qbd_ref[...] = jnp.concatenate(blocks, axis=0).astype(BF16)
        m_ref[...] = jnp.full(m_ref.shape, -jnp.inf, F32)
        l_ref[...] = jnp.zeros(l_ref.shape, F32)
        acc_ref[...] = jnp.zeros(acc_ref.shape, F32)

    def online(s, pv_fn):
        m_prev = m_ref[...]
        m_new = jnp.maximum(m_prev, jnp.max(s, axis=-1, keepdims=True))
        alpha = jnp.exp(m_prev - m_new)
        p = jnp.exp(s - m_new)
        l_ref[...] = l_ref[...] * alpha + jnp.sum(p, axis=-1, keepdims=True)
        acc_ref[...] = acc_ref[...] * alpha + pv_fn(p.astype(BF16))
        m_ref[...] = m_new

    qbd = qbd_ref[...]
    s = jnp.concatenate(
        [lax.dot_general(qbd, k_refs[j][...].astype(BF16), _NT, preferred_element_type=F32)
         for j in range(pp)], axis=1)

    def pv_pages(p):
        tot = None
        for j in range(pp):
            part = jnp.dot(p[:, j * LANES:(j + 1) * LANES], v_refs[j][...].astype(BF16),
                           preferred_element_type=F32)
            tot = part if tot is None else tot + part
        return tot

    online(s, pv_pages)

    @pl.when(g == pl.num_programs(1) - 1)
    def _():
        s2 = lax.dot_general(qbd, kn_ref[...].astype(BF16), _NT, preferred_element_type=F32)
        tok = lax.broadcasted_iota(I32, s2.shape, 0) & (t - 1)
        key = lax.broadcasted_iota(I32, s2.shape, 1)
        s2 = jnp.where(key <= tok, s2, -jnp.inf)
        vn = vn_ref[...].astype(BF16)
        online(s2, lambda p: jnp.dot(p, vn, preferred_element_type=F32))
        lam = _lam(lq1[...], lk1[...], lq2[...], lk2[...], lam_init)
        o = acc_ref[...] / l_ref[...]
        for h in range(ATT_HEADS):
            oh = o[h * 2 * t:(h + 1) * 2 * t, h * LANES:(h + 1) * LANES]
            o_ref[:, h * LANES:(h + 1) * LANES] = oh[:t] - lam * oh[t:]


def _attn_sample(page_table, lams, q, k_new, v_new, cache_k, cache_v, *, pp, lam_init, interpret=False):
    bd, t, _ = q.shape
    n_pages = page_table.shape[1]
    page = cache_k.shape[1]
    assert n_pages % pp == 0 and t & (t - 1) == 0
    rows = ATT_HEADS * 2 * t
    lam_spec = pl.BlockSpec((1, ATT_DH), lambda b, g, pt: (0, 0))
    tok_spec = pl.BlockSpec((None, t, D_ATT), lambda b, g, pt: (b, 0, 0))

    def page_spec(j):
        return pl.BlockSpec((None, page, D_ATT), lambda b, g, pt: (pt[b, g * pp + j], 0, 0))

    grid_spec = pltpu.PrefetchScalarGridSpec(
        num_scalar_prefetch=1,
        grid=(bd, n_pages // pp),
        in_specs=[lam_spec] * 4 + [tok_spec] * 3 + [page_spec(j) for j in range(pp)] * 2,
        out_specs=tok_spec,
        scratch_shapes=[pltpu.VMEM((rows, D_ATT), BF16), pltpu.VMEM((rows, 1), F32),
                        pltpu.VMEM((rows, 1), F32), pltpu.VMEM((rows, D_ATT), F32)],
    )
    return pl.pallas_call(
        functools.partial(_attn_sample_body, pp=pp, t=t, lam_init=lam_init),
        out_shape=jax.ShapeDtypeStruct((bd, t, D_ATT), F32),
        grid_spec=grid_spec,
        compiler_params=_params(("parallel", "arbitrary")),
        name="attn_sample", interpret=interpret,
    )(page_table, *lams, q, k_new, v_new, *([cache_k] * pp), *([cache_v] * pp))


def _mixer_out_body(x_ref, att_ref, u_ref, hist_ref, gsub_ref, cw_ref, cb_ref, lng_ref, lnb_ref,
                    wout_ref, o_ref, ucat_ref, conv_ref, *, tb, t, zero_first_hist, att_gain):
    hist = hist_ref[...]
    if zero_first_hist:
        hist = jnp.where(pl.program_id(1) == 0, 0.0, hist)
    ucat_ref[:, 0:CONV_HIST, :] = hist
    ucat_ref[:, CONV_HIST:CONV_HIST + t, :] = u_ref[...]
    rc = min(t, 64)
    off = CONV_HIST - (CONV_W - 1)
    for b in range(tb):
        for r0 in range(0, t, rc):
            acc = jnp.broadcast_to(cb_ref[...], (rc, D_CONV))
            for j in range(CONV_W):
                acc = acc + cw_ref[j:j + 1, :] * ucat_ref[b, r0 + j + off:r0 + j + off + rc, :]
            conv_ref[b * t + r0:b * t + r0 + rc, :] = acc
    c = conv_ref[...]
    mu = jnp.mean(c, axis=-1, keepdims=True)
    cc = c - mu
    var = jnp.mean(cc * cc, axis=-1, keepdims=True)
    cn = cc * lax.rsqrt(var + EPS) * lng_ref[...] + lnb_ref[...]
    cact = cn * jax.nn.sigmoid(cn)
    m = tb * t
    att = att_ref[...].reshape(m, D_ATT)
    gsub = gsub_ref[...] * att_gain
    a = jnp.concatenate(
        [_rmsnorm(att[:, h * LANES:(h + 1) * LANES], gsub) for h in range(ATT_HEADS)], axis=1)
    y = jnp.dot(a.astype(BF16), wout_ref[0:D_ATT, :], preferred_element_type=F32)
    y = y + jnp.dot(cact.astype(BF16), wout_ref[D_ATT:, :], preferred_element_type=F32)
    o_ref[...] = (y + x_ref[...].reshape(m, -1)).reshape(o_ref.shape)


def _mixer_out(x, att, u, hist, weights, *, tb, t, hist_from_u, att_gain, interpret=False):
    b, tt, d = x.shape
    gsub, cw, cb, lng, lnb, wout = weights
    nb = t // CONV_HIST
    if hist_from_u:
        hist_spec = pl.BlockSpec((tb, CONV_HIST, D_CONV),
                                 lambda bi, i: (bi, jnp.maximum(i * nb - 1, 0), 0))
    else:
        hist_spec = pl.BlockSpec((tb, CONV_HIST, D_CONV), lambda bi, i: (bi, 0, 0))
    row = lambda w: pl.BlockSpec((1, w), lambda bi, i: (0, 0))
    tile = lambda w: pl.BlockSpec((tb, t, w), lambda bi, i: (bi, i, 0))
    return pl.pallas_call(
        functools.partial(_mixer_out_body, tb=tb, t=t, zero_first_hist=hist_from_u, att_gain=att_gain),
        out_shape=jax.ShapeDtypeStruct(x.shape, F32),
        grid=(b // tb, tt // t),
        in_specs=[tile(d), tile(D_ATT), tile(D_CONV), hist_spec, row(LANES),
                  pl.BlockSpec((CONV_W, D_CONV), lambda bi, i: (0, 0)),
                  row(D_CONV), row(D_CONV), row(D_CONV),
                  pl.BlockSpec((d, d), lambda bi, i: (0, 0))],
        out_specs=tile(d),
        scratch_shapes=[pltpu.VMEM((tb, CONV_HIST + t, D_CONV), F32), pltpu.VMEM((tb * t, D_CONV), F32)],
        compiler_params=_params(("parallel", "arbitrary")),
        name="mixer_out", interpret=interpret,
    )(x, att, u, hist, gsub, cw, cb, lng, lnb, wout)


def _mem_kv_body(m_ref, g_ref, wk_ref, wv_ref, k_ref, v_ref):
    mn = _rmsnorm(m_ref[...], g_ref[...]).astype(BF16)
    k_ref[...] = jnp.dot(mn, wk_ref[...], preferred_element_type=F32)
    v_ref[...] = jnp.dot(mn, wv_ref[...], preferred_element_type=F32)


def _mem_kv(mem2d, g, wk, wv, *, tm, interpret=False):
    n, d = mem2d.shape
    tile = pl.BlockSpec((tm, d), lambda i: (i, 0))
    full = pl.BlockSpec((d, d), lambda i: (0, 0))
    return pl.pallas_call(
        _mem_kv_body,
        out_shape=[jax.ShapeDtypeStruct((n, d), F32)] * 2,
        grid=(n // tm,),
        in_specs=[tile, pl.BlockSpec((1, d), lambda i: (0, 0)), full, full],
        out_specs=[tile, tile],
        compiler_params=_params(("parallel",)),
        name="mem_kv", interpret=interpret,
    )(mem2d, g, wk, wv)


def _mem_attn_body(x_ref, mk_ref, mv_ref, g_ref, wq_ref, wo_ref, o_ref, *, tb, t):
    m = tb * t
    d = x_ref.shape[-1]
    dh = d // MEM_HEADS
    x = x_ref[...].reshape(m, d)
    hn = _rmsnorm(x, g_ref[...]).astype(BF16)
    q = (jnp.dot(hn, wq_ref[...], preferred_element_type=F32) * (dh ** -0.5)).astype(BF16)
    outs = []
    for b in range(tb):
        heads = []
        for h in range(MEM_HEADS):
            qh = q[b * t:(b + 1) * t, h * dh:(h + 1) * dh]
            kh = mk_ref[b, :, h * dh:(h + 1) * dh].astype(BF16)
            vh = mv_ref[b, :, h * dh:(h + 1) * dh].astype(BF16)
            s = lax.dot_general(qh, kh, _NT, preferred_element_type=F32)
            e = jnp.exp(s - jnp.max(s, axis=-1, keepdims=True))
            p = e / jnp.sum(e, axis=-1, keepdims=True)
            heads.append(jnp.dot(p.astype(BF16), vh, preferred_element_type=F32))
        outs.append(jnp.concatenate(heads, axis=1))
    o = jnp.concatenate(outs, axis=0) if tb > 1 else outs[0]
    y = jnp.dot(o.astype(BF16), wo_ref[...], preferred_element_type=F32) + x
    o_ref[...] = y.reshape(o_ref.shape)


def _mem_attn(x, mk, mv, g, wq, wo, *, tb, t, interpret=False):
    b, tt, d = x.shape
    mt = mk.shape[1]
    tile = pl.BlockSpec((tb, t, d), lambda bi, i: (bi, i, 0))
    mem = pl.BlockSpec((tb, mt, d), lambda bi, i: (bi, 0, 0))
    full = pl.BlockSpec((d, d), lambda bi, i: (0, 0))
    return pl.pallas_call(
        functools.partial(_mem_attn_body, tb=tb, t=t),
        out_shape=jax.ShapeDtypeStruct(x.shape, F32),
        grid=(b // tb, tt // t),
        in_specs=[tile, mem, mem, pl.BlockSpec((1, d), lambda bi, i: (0, 0)), full, full],
        out_specs=tile,
        compiler_params=_params(("parallel", "arbitrary")),
        name="mem_attn", interpret=interpret,
    )(x, mk, mv, g, wq, wo)


def _topk16(s, want_rank):
    r = s.shape[0]
    iota = lax.broadcasted_iota(I32, s.shape, 0)
    vals, idxs = [], []
    rank = jnp.full(s.shape, PEER_TOPK, I32) if want_rank else None
    for k in range(PEER_TOPK):
        m = jnp.max(s, axis=0, keepdims=True)
        idx = jnp.min(jnp.where(s == m, iota, r), axis=0, keepdims=True)
        hit = iota == idx
        if want_rank:
            rank = jnp.where(hit, k, rank)
        s = jnp.where(hit, -jnp.inf, s)
        vals.append(m)
        idxs.append(idx)
    return jnp.concatenate(vals, axis=0), jnp.concatenate(idxs, axis=0), rank


def _peer_route_body(x_ref, g_ref, wpq_ref, sk_ref, xn_ref, n1_ref, e1_ref, r2_ref, e2_ref, q_ref):
    xn = _rmsnorm(x_ref[...], g_ref[...]).astype(BF16)
    xn_ref[...] = xn
    q_ref[...] = jnp.dot(xn, wpq_ref[...], preferred_element_type=F32).astype(BF16)

    def head(h, carry):
        def scores(c):
            col = pl.multiple_of((h * 2 + c) * LANES, LANES)
            return lax.dot_general(sk_ref[h * 2 + c], q_ref[:, pl.ds(col, LANES)], _NT,
                                   preferred_element_type=F32)
        s1 = scores(0)
        s2 = scores(1)
        sv1, _, r1 = _topk16(s1, True)
        sv2, _, r2 = _topk16(s2, True)
        cand = jnp.concatenate([sv1[k:k + 1] + sv2 for k in range(PEER_TOPK)], axis=0)
        tv, tp, _ = _topk16(cand, False)
        k1 = tp >> 4
        n1 = jnp.zeros(s1.shape, F32)
        for k in range(PEER_TOPK):
            n1 = n1 + jnp.where(r1 == k1[k:k + 1], 1.0, 0.0)
        z = jnp.sum(jnp.exp(tv - tv[0:1]), axis=0, keepdims=True)
        n1_ref[h] = n1
        e1_ref[h] = jnp.exp(s1 - sv1[0:1])
        r2_ref[h] = r2.astype(F32)
        e2_ref[h] = jnp.exp(s2 - sv2[0:1]) / z
        return carry

    lax.fori_loop(0, PEER_HEADS, head, 0)


def _peer_route(x2d, g, wpq, sk, *, tm, interpret=False):
    n, d = x2d.shape
    dq = wpq.shape[1]
    route = pl.BlockSpec((PEER_HEADS, N_KEYS, tm), lambda i: (0, 0, i))
    route_shape = jax.ShapeDtypeStruct((PEER_HEADS, N_KEYS, n), F32)
    return pl.pallas_call(
        _peer_route_body,
        out_shape=[jax.ShapeDtypeStruct((n, d), BF16)] + [route_shape] * 4,
        grid=(n // tm,),
        in_specs=[pl.BlockSpec((tm, d), lambda i: (i, 0)),
                  pl.BlockSpec((1, d), lambda i: (0, 0)),
                  pl.BlockSpec((d, dq), lambda i: (0, 0)),
                  pl.BlockSpec(sk.shape, lambda i: (0, 0, 0))],
        out_specs=[pl.BlockSpec((tm, d), lambda i: (i, 0))] + [route] * 4,
        scratch_shapes=[pltpu.VMEM((tm, dq), BF16)],
        compiler_params=_params(("parallel",)),
        name="peer_route", interpret=interpret,
    )(x2d, g, wpq, sk)


def _peer_dense_body(x_ref, xn_ref, n1_ref, e1_ref, r2_ref, e2_ref, u_ref, vt_ref, gf_ref, y_ref,
                     acc_ref, w_ref, *, ti):
    j = pl.program_id(1)

    @pl.when(j == 0)
    def _():
        acc_ref[...] = jnp.zeros(acc_ref.shape, F32)

    ht = lax.dot_general(u_ref[...], xn_ref[...], _NT, preferred_element_type=F32)
    for il in range(ti):
        i1 = j * ti + il
        rows = slice(il * N_KEYS, (il + 1) * N_KEYS)
        gate = None
        for h in range(PEER_HEADS):
            n1 = n1_ref[h, pl.ds(i1, 1), :]
            e1 = e1_ref[h, pl.ds(i1, 1), :]
            term = jnp.where(r2_ref[h] < n1, e2_ref[h], 0.0) * e1
            gate = term if gate is None else gate + term
        w_ref[rows, :] = (_gelu(ht[rows, :]) * gate).astype(BF16)
    acc_ref[...] += jnp.dot(vt_ref[...], w_ref[...], preferred_element_type=F32)

    @pl.when(j == pl.num_programs(1) - 1)
    def _():
        y_ref[...] = _rmsnorm(x_ref[...] + acc_ref[...].T, gf_ref[...])


def _peer_dense(x2d, xn, n1, e1, r2, e2, u_bf16, vt_bf16, g_final, *, tm, ti, interpret=False):
    n, d = x2d.shape
    ne = u_bf16.shape[0]
    te = ti * N_KEYS
    tok = lambda dt: pl.BlockSpec((tm, d), lambda i, j: (i, 0))
    route = pl.BlockSpec((PEER_HEADS, N_KEYS, tm), lambda i, j: (0, 0, i))
    return pl.pallas_call(
        functools.partial(_peer_dense_body, ti=ti),
        out_shape=jax.ShapeDtypeStruct((n, d), F32),
        grid=(n // tm, ne // te),
        in_specs=[tok(F32), tok(BF16), route, route, route, route,
                  pl.BlockSpec((te, d), lambda i, j: (j, 0)),
                  pl.BlockSpec((d, te), lambda i, j: (0, j)),
                  pl.BlockSpec((1, d), lambda i, j: (0, 0))],
        out_specs=tok(F32),
        scratch_shapes=[pltpu.VMEM((d, tm), F32), pltpu.VMEM((te, tm), BF16)],
        compiler_params=_params(("parallel", "arbitrary")),
        name="peer_dense", interpret=interpret,
    )(x2d, xn, n1, e1, r2, e2, u_bf16, vt_bf16, g_final)


def _pick(n, pref):
    t = min(n, pref)
    while n % t:
        t //= 2
    return t


def _forward(x_prompt, x_sample, cache_k, cache_v, state_conv, cache_mem_k, cache_mem_v, page_table,
             mem_prompt, g_norm_mix, w_in, lam_q1, lam_k1, lam_q2, lam_k2, g_subln, conv_w, conv_b,
             conv_ln_g, conv_ln_b, w_out, g_norm_mem, g_mem_src, w_mq, w_mk, w_mv, w_mo, g_norm_peer,
             w_pq, sub_keys, u_tab, v_tab, g_final, *, interpret=False):
    depth = w_in.shape[0]
    assert depth == 1, "single layer: the final norm is fused into the PEER kernel"
    li = 0
    b, s, d = x_prompt.shape
    bd, ts, _ = x_sample.shape
    past = page_table.shape[1] * cache_k.shape[2]
    n_pool, page = cache_k.shape[1], cache_k.shape[2]
    mt = mem_prompt.shape[1]
    kw = dict(interpret=interpret)
    row = lambda a: a.reshape(1, -1)

    lam_init = 0.8 - 0.6 * math.exp(-0.3 * li)
    lams = (row(lam_q1[li]), row(lam_k1[li]), row(lam_q2[li]), row(lam_k2[li]))
    w_in_b = w_in[li].astype(BF16)
    mix_w = (row(g_subln[li]), conv_w[li], row(conv_b[li]), row(conv_ln_g[li]), row(conv_ln_b[li]),
             w_out[li].astype(BF16))

    tm_p = _pick(s, 512)
    cos_p, sin_p = _rope_tables(jnp.arange(s, dtype=I32))
    qp, kp, vp, up = _mixer_in(x_prompt.reshape(b * s, d), row(g_norm_mix[li]), w_in_b, cos_p, sin_p,
                               tm=tm_p, **kw)
    ns = bd * ts
    tm_s = _pick(ns, 512)
    cos_s, sin_s = _rope_tables(past + jnp.arange(ts, dtype=I32))
    cos_s = jnp.tile(cos_s, (tm_s // ts, 1))
    sin_s = jnp.tile(sin_s, (tm_s // ts, 1))
    qs, ks, vs, us = _mixer_in(x_sample.reshape(ns, d), row(g_norm_mix[li]), w_in_b, cos_s, sin_s,
                               tm=tm_s, **kw)
    r3 = lambda a, bb: a.reshape(bb, -1, a.shape[-1])
    qp, kp, vp, up = (r3(a, b) for a in (qp, kp, vp, up))
    qs, ks, vs, us = (r3(a, bd) for a in (qs, ks, vs, us))

    att_p = _attn_prompt(lams, qp, kp, vp, tq=_pick(s, 256), lam_init=lam_init, **kw)
    att_s = _attn_sample(page_table, lams, qs, ks, vs,
                         cache_k[li].reshape(n_pool, page, D_ATT), cache_v[li].reshape(n_pool, page, D_ATT),
                         pp=_pick(page_table.shape[1], 8), lam_init=lam_init, **kw)

    xp = _mixer_out(x_prompt, att_p, up, up, mix_w, tb=1, t=tm_p, hist_from_u=True,
                    att_gain=1.0 - lam_init, **kw)
    state = state_conv[li]
    hist_s = jnp.pad(state, ((0, 0), (CONV_HIST - state.shape[1], 0), (0, 0)))
    xs = _mixer_out(x_sample, att_s, us, hist_s, mix_w, tb=_pick(bd, 16), t=ts, hist_from_u=False,
                    att_gain=1.0 - lam_init, **kw)
    conv_p = up[:, s - (CONV_W - 1):]
    conv_s = jnp.concatenate([state, us], axis=1)[:, -(CONV_W - 1):]

    mk, mv = _mem_kv(mem_prompt.reshape(b * mt, d), row(g_mem_src[li]), w_mk[li].astype(BF16),
                     w_mv[li].astype(BF16), tm=_pick(b * mt, 512), **kw)
    mk = mk.reshape(b, mt, d)
    mv = mv.reshape(b, mt, d)
    wq_b, wo_b = w_mq[li].astype(BF16), w_mo[li].astype(BF16)
    xp = _mem_attn(xp, mk, mv, row(g_norm_mem[li]), wq_b, wo_b, tb=1, t=tm_p, **kw)
    xs = _mem_attn(xs, cache_mem_k[li].reshape(bd, mt, d), cache_mem_v[li].reshape(bd, mt, d),
                   row(g_norm_mem[li]), wq_b, wo_b, tb=_pick(bd, 8), t=ts, **kw)

    x_all = jnp.concatenate([xp.reshape(b * s, d), xs.reshape(ns, d)], axis=0)
    n_all = x_all.shape[0]
    sk = sub_keys[li].reshape(PEER_HEADS * 2, N_KEYS, -1).astype(BF16)
    xn, n1, e1, r2, e2 = _peer_route(x_all, row(g_norm_peer[li]), w_pq[li].astype(BF16), sk,
                                     tm=_pick(n_all, 256), **kw)
    y = _peer_dense(x_all, xn, n1, e1, r2, e2, u_tab[li].astype(BF16), v_tab[li].T.astype(BF16),
                    row(g_final), tm=_pick(n_all, 512), ti=_pick(N_KEYS, 8), **kw)
    y_prompt = y[:b * s].reshape(b, s, d)
    y_sample = y[b * s:].reshape(bd, ts, d)

    k4 = lambda a: a.reshape(1, a.shape[0], a.shape[1], ATT_HEADS, -1)
    return (y_prompt, y_sample, k4(kp), k4(vp), conv_p[None],
            mk.reshape(1, b, mt, MEM_HEADS, -1), mv.reshape(1, b, mt, MEM_HEADS, -1),
            k4(ks), k4(vs), conv_s[None])


def kernel(x_prompt, x_sample, cache_k, cache_v, state_conv, cache_mem_k, cache_mem_v, page_table, mem_prompt, g_norm_mix, w_in, lam_q1, lam_k1, lam_q2, lam_k2, g_subln, conv_w, conv_b, conv_ln_g, conv_ln_b, w_out, g_norm_mem, g_mem_src, w_mq, w_mk, w_mv, w_mo, g_norm_peer, w_pq, sub_keys, u_tab, v_tab, g_final):
    return _forward(x_prompt, x_sample, cache_k, cache_v, state_conv, cache_mem_k, cache_mem_v, page_table,
                    mem_prompt, g_norm_mix, w_in, lam_q1, lam_k1, lam_q2, lam_k2, g_subln, conv_w, conv_b,
                    conv_ln_g, conv_ln_b, w_out, g_norm_mem, g_mem_src, w_mq, w_mk, w_mv, w_mo, g_norm_peer,
                    w_pq, sub_keys, u_tab, v_tab, g_final)
```

```python
import functools
import math

import jax
import jax.numpy as jnp
from jax import lax
from jax.experimental import pallas as pl
from jax.experimental.pallas import tpu as pltpu

F32 = jnp.float32
BF16 = jnp.bfloat16
I32 = jnp.int32

EPS = 1e-6
ROPE_THETA = 10000.0
ATT_HEADS = 4
ATT_DH = 64
ATT_DV = 128
D_ATT = 512
D_CONV = 512
CONV_W = 31
CONV_HIST = 32
MEM_HEADS = 4
PEER_HEADS = 8
N_KEYS = 128
PEER_TOPK = 16
LANES = 128
VMEM_LIMIT = 56 * 1024 * 1024

_NT = (((1,), (1,)), ((), ()))


def _params(sem, vmem=VMEM_LIMIT):
    return pltpu.CompilerParams(dimension_semantics=sem, vmem_limit_bytes=vmem)


def _rmsnorm(x, g):
    return x * lax.rsqrt(jnp.mean(x * x, axis=-1, keepdims=True) + EPS) * g


def _gelu(x):
    return 0.5 * x * (1.0 + lax.erf(x * (2.0 ** -0.5)))


def _lam(lq1, lk1, lq2, lk2, lam_init):
    a = jnp.sum(lq1 * lk1, axis=-1, keepdims=True)
    b = jnp.sum(lq2 * lk2, axis=-1, keepdims=True)
    return jnp.exp(a) - jnp.exp(b) + lam_init


def _mixer_in_body(x_ref, g_ref, w_ref, cos_ref, sin_ref, q_ref, k_ref, v_ref, u_ref):
    xn = _rmsnorm(x_ref[...], g_ref[...])
    z = jnp.dot(xn.astype(BF16), w_ref[...], preferred_element_type=F32)
    cos = cos_ref[...]
    sin = sin_ref[...]
    lane = lax.broadcasted_iota(I32, (1, LANES), 1)
    first_half = (lane & 32) == 0

    def rope(t):
        partner = jnp.where(first_half, pltpu.roll(t, 96, 1), pltpu.roll(t, 32, 1))
        return t * cos + partner * sin

    for h in range(ATT_HEADS):
        lo = h * LANES
        q_ref[:, lo:lo + LANES] = rope(z[:, lo:lo + LANES])
        k_ref[:, lo:lo + LANES] = rope(z[:, D_ATT + lo:D_ATT + lo + LANES])
    v_ref[...] = z[:, 2 * D_ATT:3 * D_ATT]
    a = z[:, 3 * D_ATT:3 * D_ATT + D_CONV]
    gate = z[:, 3 * D_ATT + D_CONV:]
    u_ref[...] = a * jax.nn.sigmoid(gate)


def _rope_tables(pos):
    inv = ROPE_THETA ** (-jnp.arange(0, ATT_DH, 2, dtype=F32) / ATT_DH)
    ang = pos.astype(F32)[:, None] * inv[None, :]
    cos = jnp.tile(jnp.cos(ang), (1, 4))
    sin = jnp.sin(ang)
    sin = jnp.concatenate([-sin, sin, -sin, sin], axis=1)
    return cos, sin


def _mixer_in(x2d, g, w_bf16, cos, sin, *, tm, interpret=False):
    n, d = x2d.shape
    d_in = w_bf16.shape[1]
    nt = cos.shape[0] // tm
    tab = pl.BlockSpec((tm, LANES), lambda i: (i % nt, 0))
    out = pl.BlockSpec((tm, D_ATT), lambda i: (i, 0))
    return pl.pallas_call(
        _mixer_in_body,
        out_shape=[jax.ShapeDtypeStruct((n, D_ATT), F32)] * 4,
        grid=(n // tm,),
        in_specs=[pl.BlockSpec((tm, d), lambda i: (i, 0)),
                  pl.BlockSpec((1, d), lambda i: (0, 0)),
                  pl.BlockSpec((d, d_in), lambda i: (0, 0)),
                  tab, tab],
        out_specs=[out] * 4,
        compiler_params=_params(("parallel",)),
        name="mixer_in", interpret=interpret,
    )(x2d, g, w_bf16, cos, sin)


def _attn_prompt_body(lq1, lk1, lq2, lk2, q_ref, k_ref, v_ref, o_ref, *, tq, lam_init):
    qi = pl.program_id(2)
    lam = _lam(lq1[...], lk1[...], lq2[...], lk2[...], lam_init)
    q = q_ref[...] * (ATT_DH ** -0.5)
    lane = lax.broadcasted_iota(I32, (1, LANES), 1)
    qq = jnp.concatenate([jnp.where(lane < ATT_DH, q, 0.0),
                          jnp.where(lane >= ATT_DH, q, 0.0)], axis=0).astype(BF16)

    def step(j, carry, masked):
        m, l, acc = carry
        start = pl.multiple_of(j * tq, tq)
        kj = k_ref[pl.ds(start, tq), :].astype(BF16)
        vj = v_ref[pl.ds(start, tq), :].astype(BF16)
        s = lax.dot_general(qq, kj, _NT, preferred_element_type=F32)
        if masked:
            qpos = lax.broadcasted_iota(I32, s.shape, 0) & (tq - 1)
            kpos = lax.broadcasted_iota(I32, s.shape, 1)
            s = jnp.where(kpos <= qpos, s, -jnp.inf)
        m_new = jnp.maximum(m, jnp.max(s, axis=-1, keepdims=True))
        alpha = jnp.exp(m - m_new)
        p = jnp.exp(s - m_new)
        l = l * alpha + jnp.sum(p, axis=-1, keepdims=True)
        acc = acc * alpha + jnp.dot(p.astype(BF16), vj, preferred_element_type=F32)
        return m_new, l, acc

    init = (jnp.full((2 * tq, 1), -jnp.inf, F32), jnp.zeros((2 * tq, 1), F32),
            jnp.zeros((2 * tq, ATT_DV), F32))
    carry = lax.fori_loop(0, qi, lambda j, c: step(j, c, False), init)
    _, l, acc = step(qi, carry, True)
    o = acc / l
    o_ref[...] = o[:tq] - lam * o[tq:]


def _attn_prompt(lams, q, k, v, *, tq, lam_init, interpret=False):
    b, s, _ = q.shape
    assert tq & (tq - 1) == 0
    lam_spec = pl.BlockSpec((1, ATT_DH), lambda bi, h, i: (0, 0))
    kv_spec = pl.BlockSpec((None, s, LANES), lambda bi, h, i: (bi, 0, h))
    qo_spec = pl.BlockSpec((None, tq, LANES), lambda bi, h, i: (bi, i, h))
    return pl.pallas_call(
        functools.partial(_attn_prompt_body, tq=tq, lam_init=lam_init),
        out_shape=jax.ShapeDtypeStruct((b, s, D_ATT), F32),
        grid=(b, ATT_HEADS, s // tq),
        in_specs=[lam_spec] * 4 + [qo_spec, kv_spec, kv_spec],
        out_specs=qo_spec,
        compiler_params=_params(("parallel", "parallel", "arbitrary")),
        name="attn_prompt", interpret=interpret,
    )(*lams, q, k, v)


def _attn_sample_body(pt_ref, lq1, lk1, lq2, lk2, q_ref, kn_ref, vn_ref, *rest, pp, t, lam_init):
    k_refs = rest[:pp]
    v_refs = rest[pp:2 * pp]
    o_ref = rest[2 * pp]
    qm_ref, m_ref, l_ref, acc_ref = rest[2 * pp + 1:]
    g = pl.program_id(1)
    rows = ATT_HEADS * 2 * t
    head_shift = (2 * t).bit_length() - 1

    @pl.when(g == 0)
    def _():
        q = q_ref[...] * (ATT_DH ** -0.5)
        lane = lax.broadcasted_iota(I32, (t, LANES), 1)
        blocks = []
        for h in range(ATT_HEADS):
            qh = q[:, h * LANES:(h + 1) * LANES]
            blocks.append(jnp.where(lane < ATT_DH, qh, 0.0))
            blocks.append(jnp.where(lane >= ATT_DH, qh, 0.0))
        qm_ref[...] = jnp.concatenate(blocks, axis=0).astype(BF16)
        m_ref[...] = jnp.full(m_ref.shape, -jnp.inf, F32)
        l_ref[...] = jnp.zeros(l_ref.shape, F32)
        acc_ref[...] = jnp.zeros(acc_ref.shape, F32)

    def online(s_list, v_list):
        m_prev = m_ref[...]
        m_new = m_prev
        for s in s_list:
            m_new = jnp.maximum(m_new, jnp.max(s, axis=-1, keepdims=True))
        alpha = jnp.exp(m_prev - m_new)
        l = l_ref[...] * alpha
        acc = acc_ref[...] * alpha
        for s, v in zip(s_list, v_list):
            p = jnp.exp(s - m_new)
            l = l + jnp.sum(p, axis=-1, keepdims=True)
            acc = acc + jnp.dot(p.astype(BF16), v, preferred_element_type=F32)
        l_ref[...] = l
        acc_ref[...] = acc
        m_ref[...] = m_new

    qm = qm_ref[...]
    nrow = k_refs[0].shape[0]
    same_head = ((lax.broadcasted_iota(I32, (rows, nrow), 0) >> head_shift)
                 == (lax.broadcasted_iota(I32, (rows, nrow), 1) & (ATT_HEADS - 1)))
    online([jnp.where(same_head,
                      lax.dot_general(qm, k_refs[j][...].astype(BF16), _NT, preferred_element_type=F32),
                      -jnp.inf) for j in range(pp)],
           [v_refs[j][...].astype(BF16) for j in range(pp)])

    @pl.when(g == pl.num_programs(1) - 1)
    def _():
        s2 = lax.dot_general(qm, kn_ref[...].astype(BF16), _NT, preferred_element_type=F32)
        r = lax.broadcasted_iota(I32, s2.shape, 0)
        c = lax.broadcasted_iota(I32, s2.shape, 1)
        ok = ((r >> head_shift) == (c & (ATT_HEADS - 1))) & ((c >> 2) <= (r & (t - 1)))
        online([jnp.where(ok, s2, -jnp.inf)], [vn_ref[...].astype(BF16)])
        lam = _lam(lq1[...], lk1[...], lq2[...], lk2[...], lam_init)
        o = acc_ref[...] / l_ref[...]
        for h in range(ATT_HEADS):
            oh = o[h * 2 * t:(h + 1) * 2 * t]
            o_ref[:, h * LANES:(h + 1) * LANES] = oh[:t] - lam * oh[t:]


def _attn_sample(page_table, lams, q, k_new, v_new, cache_k, cache_v, li, *, pp, lam_init, interpret=False):
    bd, t, _ = q.shape
    n_pages = page_table.shape[1]
    prow = cache_k.shape[2]
    assert n_pages % pp == 0 and t & (t - 1) == 0 and ATT_HEADS == 4
    rows = ATT_HEADS * 2 * t
    lam_spec = pl.BlockSpec((1, ATT_DH), lambda b, g, pt: (0, 0))
    tok_spec = pl.BlockSpec((None, t, D_ATT), lambda b, g, pt: (b, 0, 0))
    new_spec = pl.BlockSpec((None, t * ATT_HEADS, LANES), lambda b, g, pt: (b, 0, 0))

    def page_spec(j):
        return pl.BlockSpec((None, None, prow, LANES), lambda b, g, pt: (li, pt[b, g * pp + j], 0, 0))

    grid_spec = pltpu.PrefetchScalarGridSpec(
        num_scalar_prefetch=1,
        grid=(bd, n_pages // pp),
        in_specs=[lam_spec] * 4 + [tok_spec, new_spec, new_spec] + [page_spec(j) for j in range(pp)] * 2,
        out_specs=tok_spec,
        scratch_shapes=[pltpu.VMEM((rows, LANES), BF16), pltpu.VMEM((rows, 1), F32),
                        pltpu.VMEM((rows, 1), F32), pltpu.VMEM((rows, LANES), F32)],
    )
    return pl.pallas_call(
        functools.partial(_attn_sample_body, pp=pp, t=t, lam_init=lam_init),
        out_shape=jax.ShapeDtypeStruct((bd, t, D_ATT), F32),
        grid_spec=grid_spec,
        compiler_params=_params(("parallel", "arbitrary")),
        name="attn_sample", interpret=interpret,
    )(page_table, *lams, q, k_new, v_new, *([cache_k] * pp), *([cache_v] * pp))


def _mixer_out_body(x_ref, att_ref, u_ref, hist_ref, gsub_ref, cw_ref, cb_ref, lng_ref, lnb_ref,
                    wout_ref, o_ref, ucat_ref, conv_ref, *, tb, t, zero_first_hist, att_gain):
    hist = hist_ref[...]
    if zero_first_hist:
        hist = jnp.where(pl.program_id(1) == 0, 0.0, hist)
    ucat_ref[:, 0:CONV_HIST, :] = hist
    ucat_ref[:, CONV_HIST:CONV_HIST + t, :] = u_ref[...]
    rc = min(t, 64)
    off = CONV_HIST - (CONV_W - 1)
    for b in range(tb):
        for r0 in range(0, t, rc):
            acc = jnp.broadcast_to(cb_ref[...], (rc, D_CONV))
            for j in range(CONV_W):
                acc = acc + cw_ref[j:j + 1, :] * ucat_ref[b, r0 + j + off:r0 + j + off + rc, :]
            conv_ref[b * t + r0:b * t + r0 + rc, :] = acc
    c = conv_ref[...]
    mu = jnp.mean(c, axis=-1, keepdims=True)
    cc = c - mu
    var = jnp.mean(cc * cc, axis=-1, keepdims=True)
    cn = cc * lax.rsqrt(var + EPS) * lng_ref[...] + lnb_ref[...]
    cact = cn * jax.nn.sigmoid(cn)
    m = tb * t
    att = att_ref[...].reshape(m, D_ATT)
    gsub = gsub_ref[...] * att_gain
    a = jnp.concatenate(
        [_rmsnorm(att[:, h * LANES:(h + 1) * LANES], gsub) for h in range(ATT_HEADS)], axis=1)
    y = jnp.dot(a.astype(BF16), wout_ref[0:D_ATT, :], preferred_element_type=F32)
    y = y + jnp.dot(cact.astype(BF16), wout_ref[D_ATT:, :], preferred_element_type=F32)
    o_ref[...] = (y + x_ref[...].reshape(m, -1)).reshape(o_ref.shape)


def _mixer_out(x, att, u, hist, weights, *, tb, t, hist_from_u, att_gain, interpret=False):
    b, tt, d = x.shape
    gsub, cw, cb, lng, lnb, wout = weights
    nb = t // CONV_HIST
    if hist_from_u:
        hist_spec = pl.BlockSpec((tb, CONV_HIST, D_CONV),
                                 lambda bi, i: (bi, jnp.maximum(i * nb - 1, 0), 0))
    else:
        hist_spec = pl.BlockSpec((tb, CONV_HIST, D_CONV), lambda bi, i: (bi, 0, 0))
    row = lambda w: pl.BlockSpec((1, w), lambda bi, i: (0, 0))
    tile = lambda w: pl.BlockSpec((tb, t, w), lambda bi, i: (bi, i, 0))
    return pl.pallas_call(
        functools.partial(_mixer_out_body, tb=tb, t=t, zero_first_hist=hist_from_u, att_gain=att_gain),
        out_shape=jax.ShapeDtypeStruct(x.shape, F32),
        grid=(b // tb, tt // t),
        in_specs=[tile(d), tile(D_ATT), tile(D_CONV), hist_spec, row(LANES),
                  pl.BlockSpec((CONV_W, D_CONV), lambda bi, i: (0, 0)),
                  row(D_CONV), row(D_CONV), row(D_CONV),
                  pl.BlockSpec((d, d), lambda bi, i: (0, 0))],
        out_specs=tile(d),
        scratch_shapes=[pltpu.VMEM((tb, CONV_HIST + t, D_CONV), F32), pltpu.VMEM((tb * t, D_CONV), F32)],
        compiler_params=_params(("parallel", "arbitrary")),
        name="mixer_out", interpret=interpret,
    )(x, att, u, hist, gsub, cw, cb, lng, lnb, wout)


def _mem_kv_body(m_ref, g_ref, wk_ref, wv_ref, k_ref, v_ref):
    mn = _rmsnorm(m_ref[...], g_ref[...]).astype(BF16)
    k_ref[...] = jnp.dot(mn, wk_ref[...], preferred_element_type=F32)
    v_ref[...] = jnp.dot(mn, wv_ref[...], preferred_element_type=F32)


def _mem_kv(mem2d, g, wk, wv, *, tm, interpret=False):
    n, d = mem2d.shape
    tile = pl.BlockSpec((tm, d), lambda i: (i, 0))
    full = pl.BlockSpec((d, d), lambda i: (0, 0))
    return pl.pallas_call(
        _mem_kv_body,
        out_shape=[jax.ShapeDtypeStruct((n, d), F32)] * 2,
        grid=(n // tm,),
        in_specs=[tile, pl.BlockSpec((1, d), lambda i: (0, 0)), full, full],
        out_specs=[tile, tile],
        compiler_params=_params(("parallel",)),
        name="mem_kv", interpret=interpret,
    )(mem2d, g, wk, wv)


def _mem_attn_body(x_ref, mk_ref, mv_ref, g_ref, wq_ref, wo_ref, o_ref, *, tb, t):
    m = tb * t
    d = x_ref.shape[-1]
    dh = d // MEM_HEADS
    x = x_ref[...].reshape(m, d)
    hn = _rmsnorm(x, g_ref[...]).astype(BF16)
    q = (jnp.dot(hn, wq_ref[...], preferred_element_type=F32) * (dh ** -0.5)).astype(BF16)
    outs = []
    for b in range(tb):
        heads = []
        for h in range(MEM_HEADS):
            qh = q[b * t:(b + 1) * t, h * dh:(h + 1) * dh]
            kh = mk_ref[b, :, h * dh:(h + 1) * dh].astype(BF16)
            vh = mv_ref[b, :, h * dh:(h + 1) * dh].astype(BF16)
            s = lax.dot_general(qh, kh, _NT, preferred_element_type=F32)
            e = jnp.exp(s - jnp.max(s, axis=-1, keepdims=True))
            p = e / jnp.sum(e, axis=-1, keepdims=True)
            heads.append(jnp.dot(p.astype(BF16), vh, preferred_element_type=F32))
        outs.append(jnp.concatenate(heads, axis=1))
    o = jnp.concatenate(outs, axis=0) if tb > 1 else outs[0]
    y = jnp.dot(o.astype(BF16), wo_ref[...], preferred_element_type=F32) + x
    o_ref[...] = y.reshape(o_ref.shape)


def _mem_attn(x, mk, mv, g, wq, wo, *, tb, t, interpret=False):
    b, tt, d = x.shape
    mt = mk.shape[1]
    tile = pl.BlockSpec((tb, t, d), lambda bi, i: (bi, i, 0))
    mem = pl.BlockSpec((tb, mt, d), lambda bi, i: (bi, 0, 0))
    full = pl.BlockSpec((d, d), lambda bi, i: (0, 0))
    return pl.pallas_call(
        functools.partial(_mem_attn_body, tb=tb, t=t),
        out_shape=jax.ShapeDtypeStruct(x.shape, F32),
        grid=(b // tb, tt // t),
        in_specs=[tile, mem, mem, pl.BlockSpec((1, d), lambda bi, i: (0, 0)), full, full],
        out_specs=tile,
        compiler_params=_params(("parallel", "arbitrary")),
        name="mem_attn", interpret=interpret,
    )(x, mk, mv, g, wq, wo)


def _topk16(s, want_rank):
    r = s.shape[0]
    iota = lax.broadcasted_iota(I32, s.shape, 0)
    vals, idxs = [], []
    rank = jnp.full(s.shape, PEER_TOPK, I32) if want_rank else None
    for k in range(PEER_TOPK):
        m = jnp.max(s, axis=0, keepdims=True)
        idx = jnp.min(jnp.where(s == m, iota, r), axis=0, keepdims=True)
        hit = iota == idx
        if want_rank:
            rank = jnp.where(hit, k, rank)
        s = jnp.where(hit, -jnp.inf, s)
        vals.append(m)
        idxs.append(idx)
    return jnp.concatenate(vals, axis=0), jnp.concatenate(idxs, axis=0), rank


def _route_exact(s1, s2):
    sv1, _, r1 = _topk16(s1, True)
    sv2, _, r2 = _topk16(s2, True)
    cand = jnp.concatenate([sv1[k:k + 1] + sv2 for k in range(PEER_TOPK)], axis=0)
    tv, tp, _ = _topk16(cand, False)
    k1 = tp >> 4
    n1 = jnp.zeros(s1.shape, F32)
    for k in range(PEER_TOPK):
        n1 = n1 + jnp.where(r1 == k1[k:k + 1], 1.0, 0.0)
    z = jnp.sum(jnp.exp(tv - tv[0:1]), axis=0, keepdims=True)
    return n1, r2, z


def _topk16_distinct(s, want_rank):
    vals = []
    rank = jnp.full(s.shape, PEER_TOPK, I32) if want_rank else None
    for k in range(PEER_TOPK):
        m = jnp.max(s, axis=0, keepdims=True)
        hit = s == m
        if want_rank:
            rank = jnp.where(hit, k, rank)
        s = jnp.where(hit, -jnp.inf, s)
        vals.append(m)
    return vals, rank, s


_CAND_GROUPS = (
    ((0, 0, 0, 8),),
    ((0, 0, 8, 8),),
    ((1, 0, 0, 8),),
    ((2, 0, 0, 5), (4, 5, 0, 3)),
    ((3, 0, 0, 4), (5, 4, 0, 2), (6, 6, 0, 2)),
    ((7, 0, 0, 2),) + tuple((8 + i, 2 + i, 0, 1) for i in range(6)),
    ((14, 0, 0, 1), (15, 1, 0, 1)),
)


def _route_distinct(s1, s2):
    t = s1.shape[1]
    sv1, r1, _ = _topk16_distinct(s1, True)
    sv2, r2, _ = _topk16_distinct(s2, True)
    sv2_lo = jnp.concatenate(sv2[:8], axis=0)
    sv2_hi = jnp.concatenate(sv2[8:], axis=0)
    row = lax.broadcasted_iota(I32, (8, t), 0)
    groups = []
    for pieces in _CAND_GROUPS:
        val = jnp.full((8, t), -jnp.inf, F32)
        for k1, off, k2, nk in pieces:
            src = sv2_hi if k2 else sv2_lo
            if off:
                src = pltpu.roll(src, off, 0)
            val = jnp.where((row >= off) & (row < off + nk), sv1[k1] + src, val)
        groups.append(val)
    cand = jnp.concatenate(groups, axis=0)
    tv, _, left = _topk16_distinct(cand, False)
    won = jnp.where(left != cand, 1.0, 0.0)
    cnt = [jnp.zeros((1, t), F32) for _ in range(PEER_TOPK)]
    for gi, pieces in enumerate(_CAND_GROUPS):
        w = won[gi * 8:(gi + 1) * 8]
        for k1, off, _, nk in pieces:
            cnt[k1] = cnt[k1] + jnp.sum(jnp.where((row >= off) & (row < off + nk), w, 0.0),
                                        axis=0, keepdims=True)
    n1 = jnp.zeros(s1.shape, F32)
    for k in range(PEER_TOPK):
        n1 = jnp.where(r1 == k, cnt[k], n1)
    z = jnp.zeros((1, t), F32)
    for k in range(PEER_TOPK):
        z = z + jnp.exp(tv[k] - tv[0])
    picked = lambda r: jnp.sum(jnp.where(r < PEER_TOPK, 1.0, 0.0), axis=0, keepdims=True)
    total = cnt[0]
    for k in range(1, PEER_TOPK):
        total = total + cnt[k]
    tied = (picked(r1) != PEER_TOPK) | (picked(r2) != PEER_TOPK) | (total != PEER_TOPK)
    return n1, r2, z, sv1[0], sv2[0], tied


def _peer_route_body(x_ref, g_ref, wpq_ref, sk_ref, xn_ref, n1_ref, e1_ref, r2_ref, e2_ref, q_ref):
    xn = _rmsnorm(x_ref[...], g_ref[...]).astype(BF16)
    xn_ref[...] = xn
    q_ref[...] = jnp.dot(xn, wpq_ref[...], preferred_element_type=F32).astype(BF16)

    def head(h, carry):
        def scores(c):
            col = pl.multiple_of((h * 2 + c) * LANES, LANES)
            return lax.dot_general(sk_ref[h * 2 + c], q_ref[:, pl.ds(col, LANES)], _NT,
                                   preferred_element_type=F32)
        s1 = scores(0)
        s2 = scores(1)
        n1, r2, z, top1, top2, tied = _route_distinct(s1, s2)
        ex2 = jnp.exp(s2 - top2)
        n1_ref[h] = n1
        e1_ref[h] = jnp.exp(s1 - top1)
        r2_ref[h] = r2.astype(BF16)
        e2_ref[h] = (ex2 / z).astype(BF16)

        @pl.when(jnp.max(jnp.where(tied, 1.0, 0.0)) > 0.5)
        def _():
            n1x, r2x, zx = _route_exact(s1, s2)
            n1_ref[h] = n1x
            r2_ref[h] = r2x.astype(BF16)
            e2_ref[h] = (ex2 / zx).astype(BF16)

        return carry

    lax.fori_loop(0, PEER_HEADS, head, 0)


def _peer_route(x2d, g, wpq, sk, *, tm, interpret=False):
    n, d = x2d.shape
    dq = wpq.shape[1]
    route = pl.BlockSpec((PEER_HEADS, N_KEYS, tm), lambda i: (0, 0, i))
    route_f32 = jax.ShapeDtypeStruct((PEER_HEADS, N_KEYS, n), F32)
    route_b16 = jax.ShapeDtypeStruct((PEER_HEADS, N_KEYS, n), BF16)
    return pl.pallas_call(
        _peer_route_body,
        out_shape=[jax.ShapeDtypeStruct((n, d), BF16), route_f32, route_f32, route_b16, route_b16],
        grid=(n // tm,),
        in_specs=[pl.BlockSpec((tm, d), lambda i: (i, 0)),
                  pl.BlockSpec((1, d), lambda i: (0, 0)),
                  pl.BlockSpec((d, dq), lambda i: (0, 0)),
                  pl.BlockSpec(sk.shape, lambda i: (0, 0, 0))],
        out_specs=[pl.BlockSpec((tm, d), lambda i: (i, 0))] + [route] * 4,
        scratch_shapes=[pltpu.VMEM((tm, dq), BF16)],
        compiler_params=_params(("parallel",)),
        name="peer_route", interpret=interpret,
    )(x2d, g, wpq, sk)


def _peer_dense_body(x_ref, xn_ref, n1_ref, e1_ref, r2_ref, e2_ref, u_ref, vt_ref, gf_ref, y_ref,
                     acc_ref, w_ref, *, ti):
    j = pl.program_id(1)

    @pl.when(j == 0)
    def _():
        acc_ref[...] = jnp.zeros(acc_ref.shape, F32)

    ht = lax.dot_general(u_ref[...], xn_ref[...], _NT, preferred_element_type=F32)
    for il in range(ti):
        i1 = j * ti + il
        rows = slice(il * N_KEYS, (il + 1) * N_KEYS)
        gate = None
        for h in range(PEER_HEADS):
            n1 = n1_ref[h, pl.ds(i1, 1), :].astype(BF16)
            e1 = e1_ref[h, pl.ds(i1, 1), :].astype(BF16)
            term = jnp.where(r2_ref[h] < n1, e2_ref[h], jnp.zeros((), BF16)) * e1
            gate = term if gate is None else gate + term
        w_ref[rows, :] = _gelu(ht[rows, :]).astype(BF16) * gate
    acc_ref[...] += jnp.dot(vt_ref[...], w_ref[...], preferred_element_type=F32)

    @pl.when(j == pl.num_programs(1) - 1)
    def _():
        y_ref[...] = _rmsnorm(x_ref[...] + acc_ref[...].T, gf_ref[...])


def _peer_dense(x2d, xn, n1, e1, r2, e2, u_bf16, vt_bf16, g_final, *, tm, ti, interpret=False):
    n, d = x2d.shape
    ne = u_bf16.shape[0]
    te = ti * N_KEYS
    tok = lambda dt: pl.BlockSpec((tm, d), lambda i, j: (i, 0))
    route = pl.BlockSpec((PEER_HEADS, N_KEYS, tm), lambda i, j: (0, 0, i))
    return pl.pallas_call(
        functools.partial(_peer_dense_body, ti=ti),
        out_shape=jax.ShapeDtypeStruct((n, d), F32),
        grid=(n // tm, ne // te),
        in_specs=[tok(F32), tok(BF16), route, route, route, route,
                  pl.BlockSpec((te, d), lambda i, j: (j, 0)),
                  pl.BlockSpec((d, te), lambda i, j: (0, j)),
                  pl.BlockSpec((1, d), lambda i, j: (0, 0))],
        out_specs=tok(F32),
        scratch_shapes=[pltpu.VMEM((d, tm), F32), pltpu.VMEM((te, tm), BF16)],
        compiler_params=_params(("parallel", "arbitrary")),
        name="peer_dense", interpret=interpret,
    )(x2d, xn, n1, e1, r2, e2, u_bf16, vt_bf16, g_final)


def _pick(n, pref):
    t = min(n, pref)
    while n % t:
        t //= 2
    return t


def _forward(x_prompt, x_sample, cache_k, cache_v, state_conv, cache_mem_k, cache_mem_v, page_table,
             mem_prompt, g_norm_mix, w_in, lam_q1, lam_k1, lam_q2, lam_k2, g_subln, conv_w, conv_b,
             conv_ln_g, conv_ln_b, w_out, g_norm_mem, g_mem_src, w_mq, w_mk, w_mv, w_mo, g_norm_peer,
             w_pq, sub_keys, u_tab, v_tab, g_final, *, interpret=False):
    depth = w_in.shape[0]
    assert depth == 1, "single layer: the final norm is fused into the PEER kernel"
    li = 0
    b, s, d = x_prompt.shape
    bd, ts, _ = x_sample.shape
    past = page_table.shape[1] * cache_k.shape[2]
    n_pool, page = cache_k.shape[1], cache_k.shape[2]
    mt = mem_prompt.shape[1]
    kw = dict(interpret=interpret)
    row = lambda a: a.reshape(1, -1)

    lam_init = 0.8 - 0.6 * math.exp(-0.3 * li)
    lams = (row(lam_q1[li]), row(lam_k1[li]), row(lam_q2[li]), row(lam_k2[li]))
    w_in_b = w_in[li].astype(BF16)
    mix_w = (row(g_subln[li]), conv_w[li], row(conv_b[li]), row(conv_ln_g[li]), row(conv_ln_b[li]),
             w_out[li].astype(BF16))

    tm_p = _pick(s, 512)
    cos_p, sin_p = _rope_tables(jnp.arange(s, dtype=I32))
    qp, kp, vp, up = _mixer_in(x_prompt.reshape(b * s, d), row(g_norm_mix[li]), w_in_b, cos_p, sin_p,
                               tm=tm_p, **kw)
    ns = bd * ts
    tm_s = _pick(ns, 512)
    cos_s, sin_s = _rope_tables(past + jnp.arange(ts, dtype=I32))
    cos_s = jnp.tile(cos_s, (tm_s // ts, 1))
    sin_s = jnp.tile(sin_s, (tm_s // ts, 1))
    qs, ks, vs, us = _mixer_in(x_sample.reshape(ns, d), row(g_norm_mix[li]), w_in_b, cos_s, sin_s,
                               tm=tm_s, **kw)
    r3 = lambda a, bb: a.reshape(bb, -1, a.shape[-1])
    qp, kp, vp, up = (r3(a, b) for a in (qp, kp, vp, up))
    qs, ks, vs, us = (r3(a, bd) for a in (qs, ks, vs, us))

    att_p = _attn_prompt(lams, qp, kp, vp, tq=_pick(s, 256), lam_init=lam_init, **kw)
    flat = lambda c: c.reshape(c.shape[0], n_pool, page * ATT_HEADS, LANES)
    att_s = _attn_sample(page_table, lams, qs, ks.reshape(bd, ts * ATT_HEADS, LANES),
                         vs.reshape(bd, ts * ATT_HEADS, LANES), flat(cache_k), flat(cache_v), li,
                         pp=_pick(page_table.shape[1], 8), lam_init=lam_init, **kw)

    xp = _mixer_out(x_prompt, att_p, up, up, mix_w, tb=1, t=tm_p, hist_from_u=True,
                    att_gain=1.0 - lam_init, **kw)
    state = state_conv[li]
    hist_s = jnp.pad(state, ((0, 0), (CONV_HIST - state.shape[1], 0), (0, 0)))
    xs = _mixer_out(x_sample, att_s, us, hist_s, mix_w, tb=_pick(bd, 16), t=ts, hist_from_u=False,
                    att_gain=1.0 - lam_init, **kw)
    conv_p = up[:, s - (CONV_W - 1):]
    conv_s = jnp.concatenate([state, us], axis=1)[:, -(CONV_W - 1):]

    mk, mv = _mem_kv(mem_prompt.reshape(b * mt, d), row(g_mem_src[li]), w_mk[li].astype(BF16),
                     w_mv[li].astype(BF16), tm=_pick(b * mt, 512), **kw)
    mk = mk.reshape(b, mt, d)
    mv = mv.reshape(b, mt, d)
    wq_b, wo_b = w_mq[li].astype(BF16), w_mo[li].astype(BF16)
    xp = _mem_attn(xp, mk, mv, row(g_norm_mem[li]), wq_b, wo_b, tb=1, t=tm_p, **kw)
    xs = _mem_attn(xs, cache_mem_k[li].reshape(bd, mt, d), cache_mem_v[li].reshape(bd, mt, d),
                   row(g_norm_mem[li]), wq_b, wo_b, tb=_pick(bd, 8), t=ts, **kw)

    x_all = jnp.concatenate([xp.reshape(b * s, d), xs.reshape(ns, d)], axis=0)
    n_all = x_all.shape[0]
    sk = sub_keys[li].reshape(PEER_HEADS * 2, N_KEYS, -1).astype(BF16)
    xn, n1, e1, r2, e2 = _peer_route(x_all, row(g_norm_peer[li]), w_pq[li].astype(BF16), sk,
                                     tm=_pick(n_all, 256), **kw)
    y = _peer_dense(x_all, xn, n1, e1, r2, e2, u_tab[li].astype(BF16), v_tab[li].T.astype(BF16),
                    row(g_final), tm=_pick(n_all, 512), ti=_pick(N_KEYS, 8), **kw)
    y_prompt = y[:b * s].reshape(b, s, d)
    y_sample = y[b * s:].reshape(bd, ts, d)

    k4 = lambda a: a.reshape(1, a.shape[0], a.shape[1], ATT_HEADS, -1)
    return (y_prompt, y_sample, k4(kp), k4(vp), conv_p[None],
            mk.reshape(1, b, mt, MEM_HEADS, -1), mv.reshape(1, b, mt, MEM_HEADS, -1),
            k4(ks), k4(vs), conv_s[None])


def kernel(x_prompt, x_sample, cache_k, cache_v, state_conv, cache_mem_k, cache_mem_v, page_table, mem_prompt, g_norm_mix, w_in, lam_q1, lam_k1, lam_q2, lam_k2, g_subln, conv_w, conv_b, conv_ln_g, conv_ln_b, w_out, g_norm_mem, g_mem_src, w_mq, w_mk, w_mv, w_mo, g_norm_peer, w_pq, sub_keys, u_tab, v_tab, g_final):
    return _forward(x_prompt, x_sample, cache_k, cache_v, state_conv, cache_mem_k, cache_mem_v, page_table,
                    mem_prompt, g_norm_mix, w_in, lam_q1, lam_k1, lam_q2, lam_k2, g_subln, conv_w, conv_b,
                    conv_ln_g, conv_ln_b, w_out, g_norm_mem, g_mem_src, w_mq, w_mk, w_mv, w_mo, g_norm_peer,
                    w_pq, sub_keys, u_tab, v_tab, g_final)
```

```python
import functools
import math

import jax
import jax.numpy as jnp
from jax import lax
from jax.experimental import pallas as pl
from jax.experimental.pallas import tpu as pltpu

F32 = jnp.float32
BF16 = jnp.bfloat16
I32 = jnp.int32

EPS = 1e-6
ROPE_THETA = 10000.0
ATT_HEADS = 4
ATT_DH = 64
ATT_DV = 128
D_ATT = 512
D_CONV = 512
CONV_W = 31
CONV_HIST = 32
MEM_HEADS = 4
PEER_HEADS = 8
N_KEYS = 128
PEER_TOPK = 16
LANES = 128
VMEM_LIMIT = 56 * 1024 * 1024

_NT = (((1,), (1,)), ((), ()))


def _params(sem, vmem=VMEM_LIMIT):
    return pltpu.CompilerParams(dimension_semantics=sem, vmem_limit_bytes=vmem)


def _rmsnorm(x, g):
    return x * lax.rsqrt(jnp.mean(x * x, axis=-1, keepdims=True) + EPS) * g


def _gelu(x):
    return 0.5 * x * (1.0 + lax.erf(x * (2.0 ** -0.5)))


def _lam(lq1, lk1, lq2, lk2, lam_init):
    a = jnp.sum(lq1 * lk1, axis=-1, keepdims=True)
    b = jnp.sum(lq2 * lk2, axis=-1, keepdims=True)
    return jnp.exp(a) - jnp.exp(b) + lam_init


def _mixer_in_body(x_ref, g_ref, w_ref, cos_ref, sin_ref, q_ref, k_ref, v_ref, u_ref):
    xn = _rmsnorm(x_ref[...], g_ref[...])
    z = jnp.dot(xn.astype(BF16), w_ref[...], preferred_element_type=F32)
    cos = cos_ref[...]
    sin = sin_ref[...]
    lane = lax.broadcasted_iota(I32, (1, LANES), 1)
    first_half = (lane & 32) == 0

    def rope(t):
        partner = jnp.where(first_half, pltpu.roll(t, 96, 1), pltpu.roll(t, 32, 1))
        return t * cos + partner * sin

    for h in range(ATT_HEADS):
        lo = h * LANES
        q_ref[:, lo:lo + LANES] = rope(z[:, lo:lo + LANES])
        k_ref[:, lo:lo + LANES] = rope(z[:, D_ATT + lo:D_ATT + lo + LANES])
    v_ref[...] = z[:, 2 * D_ATT:3 * D_ATT]
    a = z[:, 3 * D_ATT:3 * D_ATT + D_CONV]
    gate = z[:, 3 * D_ATT + D_CONV:]
    u_ref[...] = a * jax.nn.sigmoid(gate)


def _mixer_in_t_body(x_ref, g_ref, w_ref, wt_ref, cos_ref, sin_ref, cost_ref, sint_ref,
                     qt_ref, k_ref, kb_ref, v_ref, vt_ref, u_ref):
    xn = _rmsnorm(x_ref[...], g_ref[...]).astype(BF16)
    z = jnp.dot(xn, w_ref[...], preferred_element_type=F32)
    zt = lax.dot_general(wt_ref[...], xn, _NT, preferred_element_type=F32)
    cos = cos_ref[...]
    sin = sin_ref[...]
    lane = lax.broadcasted_iota(I32, (1, LANES), 1)
    first_half = (lane & 32) == 0
    for h in range(ATT_HEADS):
        t = z[:, h * LANES:(h + 1) * LANES]
        partner = jnp.where(first_half, pltpu.roll(t, 96, 1), pltpu.roll(t, 32, 1))
        kh = t * cos + partner * sin
        k_ref[:, h * LANES:(h + 1) * LANES] = kh
        kb_ref[:, h * LANES:(h + 1) * LANES] = kh.astype(BF16)
    v_ref[...] = z[:, D_ATT:2 * D_ATT]
    u_ref[...] = z[:, 2 * D_ATT:2 * D_ATT + D_CONV] * jax.nn.sigmoid(z[:, 2 * D_ATT + D_CONV:])
    ct = cost_ref[...]
    st = sint_ref[...]
    half = ATT_DH // 2
    qscale = (ATT_DH ** -0.5) * math.log2(math.e)
    for blk in range(2 * ATT_HEADS):
        r0 = blk * ATT_DH
        x1 = zt[r0:r0 + half]
        x2 = zt[r0 + half:r0 + ATT_DH]
        qt_ref[r0:r0 + half, :] = ((x1 * ct - x2 * st) * qscale).astype(BF16)
        qt_ref[r0 + half:r0 + ATT_DH, :] = ((x2 * ct + x1 * st) * qscale).astype(BF16)
    vt_ref[...] = zt[D_ATT:].astype(BF16)


def _mixer_in_t(x2d, g, w_bf16, wt_bf16, tabs, *, b, tm, interpret=False):
    n, d = x2d.shape
    s = n // b
    nt = s // tm
    cos, sin, cost, sint = tabs
    tab = pl.BlockSpec((tm, LANES), lambda i: (i % nt, 0))
    tabt = pl.BlockSpec((ATT_DH // 2, tm), lambda i: (0, i % nt))
    nat = pl.BlockSpec((tm, D_ATT), lambda i: (i, 0))
    fmaj = pl.BlockSpec((None, D_ATT, tm), lambda i: (i // nt, 0, i % nt))
    nat_f32 = jax.ShapeDtypeStruct((n, D_ATT), F32)
    fmaj_b16 = jax.ShapeDtypeStruct((b, D_ATT, s), BF16)
    return pl.pallas_call(
        _mixer_in_t_body,
        out_shape=[fmaj_b16, nat_f32, jax.ShapeDtypeStruct((n, D_ATT), BF16), nat_f32, fmaj_b16, nat_f32],
        grid=(n // tm,),
        in_specs=[pl.BlockSpec((tm, d), lambda i: (i, 0)),
                  pl.BlockSpec((1, d), lambda i: (0, 0)),
                  pl.BlockSpec(w_bf16.shape, lambda i: (0, 0)),
                  pl.BlockSpec(wt_bf16.shape, lambda i: (0, 0)),
                  tab, tab, tabt, tabt],
        out_specs=[fmaj, nat, nat, nat, fmaj, nat],
        compiler_params=_params(("parallel",)),
        name="mixer_in_t", interpret=interpret,
    )(x2d, g, w_bf16, wt_bf16, cos, sin, cost, sint)


def _rope_tables(pos):
    inv = ROPE_THETA ** (-jnp.arange(0, ATT_DH, 2, dtype=F32) / ATT_DH)
    ang = pos.astype(F32)[:, None] * inv[None, :]
    cos = jnp.tile(jnp.cos(ang), (1, 4))
    sin = jnp.sin(ang)
    sin4 = jnp.concatenate([-sin, sin, -sin, sin], axis=1)
    return cos, sin4, jnp.cos(ang).T, sin.T


def _mixer_in(x2d, g, w_bf16, cos, sin, *, tm, interpret=False):
    n, d = x2d.shape
    d_in = w_bf16.shape[1]
    nt = cos.shape[0] // tm
    tab = pl.BlockSpec((tm, LANES), lambda i: (i % nt, 0))
    out = pl.BlockSpec((tm, D_ATT), lambda i: (i, 0))
    return pl.pallas_call(
        _mixer_in_body,
        out_shape=[jax.ShapeDtypeStruct((n, D_ATT), F32)] * 4,
        grid=(n // tm,),
        in_specs=[pl.BlockSpec((tm, d), lambda i: (i, 0)),
                  pl.BlockSpec((1, d), lambda i: (0, 0)),
                  pl.BlockSpec((d, d_in), lambda i: (0, 0)),
                  tab, tab],
        out_specs=[out] * 4,
        compiler_params=_params(("parallel",)),
        name="mixer_in", interpret=interpret,
    )(x2d, g, w_bf16, cos, sin)


def _attn_prompt_body(lq1, lk1, lq2, lk2, qt_ref, k_ref, vt_ref, o_ref, *, tq, lam_init):
    qi = pl.program_id(2)
    lam = _lam(lq1[...], lk1[...], lq2[...], lk2[...], lam_init)
    heads = qt_ref.shape[0] // LANES
    feat = lax.broadcasted_iota(I32, (LANES, tq), 0)
    zero = jnp.zeros((), BF16)
    qq = []
    for g in range(heads):
        qt = qt_ref[g * LANES:(g + 1) * LANES, :]
        qq.append(jnp.concatenate([jnp.where(feat < ATT_DH, qt, zero),
                                   jnp.where(feat >= ATT_DH, qt, zero)], axis=1))

    def step(j, carry, masked):
        start = pl.multiple_of(j * tq, tq)
        out = []
        scores = [jnp.dot(k_ref[pl.ds(start, tq), g * LANES:(g + 1) * LANES], qq[g],
                          preferred_element_type=F32) for g in range(heads)]
        for g in range(heads):
            m, l, acc = carry[g]
            vj = vt_ref[g * LANES:(g + 1) * LANES, pl.ds(start, tq)]
            s = scores[g]
            if masked:
                kpos = lax.broadcasted_iota(I32, s.shape, 0)
                qpos = lax.broadcasted_iota(I32, s.shape, 1) & (tq - 1)
                s = jnp.where(kpos <= qpos, s, -jnp.inf)
            m_new = jnp.maximum(m, jnp.max(s, axis=0, keepdims=True))
            alpha = jnp.exp2(m - m_new)
            p = jnp.exp2(s - m_new)
            l = l * alpha + jnp.sum(p, axis=0, keepdims=True)
            acc = acc * alpha + jnp.dot(vj, p.astype(BF16), preferred_element_type=F32)
            out.append((m_new, l, acc))
        return tuple(out)

    init = tuple((jnp.full((1, 2 * tq), -jnp.inf, F32), jnp.zeros((1, 2 * tq), F32),
                  jnp.zeros((ATT_DV, 2 * tq), F32)) for _ in range(heads))
    carry = lax.fori_loop(0, qi, lambda j, c: step(j, c, False), init)
    carry = step(qi, carry, True)
    for g in range(heads):
        _, l, acc = carry[g]
        o = acc / l
        o_ref[:, g * LANES:(g + 1) * LANES] = (o[:, :tq] - lam * o[:, tq:]).T


def _attn_prompt(lams, qt, kb, vt, *, tq, heads, lam_init, interpret=False):
    b, s, _ = kb.shape
    assert tq & (tq - 1) == 0 and ATT_HEADS % heads == 0
    lam_spec = pl.BlockSpec((1, ATT_DH), lambda bi, h, i: (0, 0))
    w = heads * LANES
    return pl.pallas_call(
        functools.partial(_attn_prompt_body, tq=tq, lam_init=lam_init),
        out_shape=jax.ShapeDtypeStruct((b, s, D_ATT), F32),
        grid=(b, ATT_HEADS // heads, s // tq),
        in_specs=[lam_spec] * 4 + [pl.BlockSpec((None, w, tq), lambda bi, h, i: (bi, h, i)),
                                   pl.BlockSpec((None, s, w), lambda bi, h, i: (bi, 0, h)),
                                   pl.BlockSpec((None, w, s), lambda bi, h, i: (bi, h, 0))],
        out_specs=pl.BlockSpec((None, tq, w), lambda bi, h, i: (bi, i, h)),
        compiler_params=_params(("parallel", "parallel", "arbitrary")),
        name="attn_prompt", interpret=interpret,
    )(*lams, qt, kb, vt)


def _attn_sample_body(pt_ref, lq1, lk1, lq2, lk2, q_ref, kn_ref, vn_ref, *rest, pp, t, lam_init):
    k_refs = rest[:pp]
    v_refs = rest[pp:2 * pp]
    o_ref = rest[2 * pp]
    qm_ref, m_ref, l_ref, acc_ref = rest[2 * pp + 1:]
    g = pl.program_id(1)
    rows = ATT_HEADS * 2 * t
    head_shift = (2 * t).bit_length() - 1

    @pl.when(g == 0)
    def _():
        q = q_ref[...] * (ATT_DH ** -0.5)
        lane = lax.broadcasted_iota(I32, (t, LANES), 1)
        blocks = []
        for h in range(ATT_HEADS):
            qh = q[:, h * LANES:(h + 1) * LANES]
            blocks.append(jnp.where(lane < ATT_DH, qh, 0.0))
            blocks.append(jnp.where(lane >= ATT_DH, qh, 0.0))
        qm_ref[...] = jnp.concatenate(blocks, axis=0).astype(BF16)
        m_ref[...] = jnp.full(m_ref.shape, -jnp.inf, F32)
        l_ref[...] = jnp.zeros(l_ref.shape, F32)
        acc_ref[...] = jnp.zeros(acc_ref.shape, F32)

    def online(s_list, v_list):
        m_prev = m_ref[...]
        m_new = m_prev
        for s in s_list:
            m_new = jnp.maximum(m_new, jnp.max(s, axis=-1, keepdims=True))
        alpha = jnp.exp(m_prev - m_new)
        l = l_ref[...] * alpha
        acc = acc_ref[...] * alpha
        for s, v in zip(s_list, v_list):
            p = jnp.exp(s - m_new)
            l = l + jnp.sum(p, axis=-1, keepdims=True)
            acc = acc + jnp.dot(p.astype(BF16), v, preferred_element_type=F32)
        l_ref[...] = l
        acc_ref[...] = acc
        m_ref[...] = m_new

    qm = qm_ref[...]
    nrow = k_refs[0].shape[0]
    same_head = ((lax.broadcasted_iota(I32, (rows, nrow), 0) >> head_shift)
                 == (lax.broadcasted_iota(I32, (rows, nrow), 1) & (ATT_HEADS - 1)))
    online([jnp.where(same_head,
                      lax.dot_general(qm, k_refs[j][...].astype(BF16), _NT, preferred_element_type=F32),
                      -jnp.inf) for j in range(pp)],
           [v_refs[j][...].astype(BF16) for j in range(pp)])

    @pl.when(g == pl.num_programs(1) - 1)
    def _():
        s2 = lax.dot_general(qm, kn_ref[...].astype(BF16), _NT, preferred_element_type=F32)
        r = lax.broadcasted_iota(I32, s2.shape, 0)
        c = lax.broadcasted_iota(I32, s2.shape, 1)
        ok = ((r >> head_shift) == (c & (ATT_HEADS - 1))) & ((c >> 2) <= (r & (t - 1)))
        online([jnp.where(ok, s2, -jnp.inf)], [vn_ref[...].astype(BF16)])
        lam = _lam(lq1[...], lk1[...], lq2[...], lk2[...], lam_init)
        o = acc_ref[...] / l_ref[...]
        for h in range(ATT_HEADS):
            oh = o[h * 2 * t:(h + 1) * 2 * t]
            o_ref[:, h * LANES:(h + 1) * LANES] = oh[:t] - lam * oh[t:]


def _attn_sample(page_table, lams, q, k_new, v_new, cache_k, cache_v, li, *, pp, lam_init, interpret=False):
    bd, t, _ = q.shape
    n_pages = page_table.shape[1]
    prow = cache_k.shape[2]
    assert n_pages % pp == 0 and t & (t - 1) == 0 and ATT_HEADS == 4
    rows = ATT_HEADS * 2 * t
    lam_spec = pl.BlockSpec((1, ATT_DH), lambda b, g, pt: (0, 0))
    tok_spec = pl.BlockSpec((None, t, D_ATT), lambda b, g, pt: (b, 0, 0))
    new_spec = pl.BlockSpec((None, t * ATT_HEADS, LANES), lambda b, g, pt: (b, 0, 0))

    def page_spec(j):
        return pl.BlockSpec((None, None, prow, LANES), lambda b, g, pt: (li, pt[b, g * pp + j], 0, 0))

    grid_spec = pltpu.PrefetchScalarGridSpec(
        num_scalar_prefetch=1,
        grid=(bd, n_pages // pp),
        in_specs=[lam_spec] * 4 + [tok_spec, new_spec, new_spec] + [page_spec(j) for j in range(pp)] * 2,
        out_specs=tok_spec,
        scratch_shapes=[pltpu.VMEM((rows, LANES), BF16), pltpu.VMEM((rows, 1), F32),
                        pltpu.VMEM((rows, 1), F32), pltpu.VMEM((rows, LANES), F32)],
    )
    return pl.pallas_call(
        functools.partial(_attn_sample_body, pp=pp, t=t, lam_init=lam_init),
        out_shape=jax.ShapeDtypeStruct((bd, t, D_ATT), F32),
        grid_spec=grid_spec,
        compiler_params=_params(("parallel", "arbitrary")),
        name="attn_sample", interpret=interpret,
    )(page_table, *lams, q, k_new, v_new, *([cache_k] * pp), *([cache_v] * pp))


def _mixer_out_body(x_ref, att_ref, u_ref, hist_ref, gsub_ref, cw_ref, cb_ref, lng_ref, lnb_ref,
                    wout_ref, o_ref, ucat_ref, conv_ref, *, tb, t, zero_first_hist, att_gain):
    hist = hist_ref[...]
    if zero_first_hist:
        hist = jnp.where(pl.program_id(1) == 0, 0.0, hist)
    ucat_ref[:, 0:CONV_HIST, :] = hist
    ucat_ref[:, CONV_HIST:CONV_HIST + t, :] = u_ref[...]
    rc = min(t, 64)
    off = CONV_HIST - (CONV_W - 1)
    for b in range(tb):
        for r0 in range(0, t, rc):
            acc = jnp.broadcast_to(cb_ref[...], (rc, D_CONV))
            for j in range(CONV_W):
                acc = acc + cw_ref[j:j + 1, :] * ucat_ref[b, r0 + j + off:r0 + j + off + rc, :]
            conv_ref[b * t + r0:b * t + r0 + rc, :] = acc
    c = conv_ref[...]
    mu = jnp.mean(c, axis=-1, keepdims=True)
    cc = c - mu
    var = jnp.mean(cc * cc, axis=-1, keepdims=True)
    cn = cc * lax.rsqrt(var + EPS) * lng_ref[...] + lnb_ref[...]
    cact = cn * jax.nn.sigmoid(cn)
    m = tb * t
    att = att_ref[...].reshape(m, D_ATT)
    gsub = gsub_ref[...] * att_gain
    a = jnp.concatenate(
        [_rmsnorm(att[:, h * LANES:(h + 1) * LANES], gsub) for h in range(ATT_HEADS)], axis=1)
    y = jnp.dot(a.astype(BF16), wout_ref[0:D_ATT, :], preferred_element_type=F32)
    y = y + jnp.dot(cact.astype(BF16), wout_ref[D_ATT:, :], preferred_element_type=F32)
    o_ref[...] = (y + x_ref[...].reshape(m, -1)).reshape(o_ref.shape)


def _mixer_out(x, att, u, hist, weights, *, tb, t, hist_from_u, att_gain, interpret=False):
    b, tt, d = x.shape
    gsub, cw, cb, lng, lnb, wout = weights
    nb = t // CONV_HIST
    if hist_from_u:
        hist_spec = pl.BlockSpec((tb, CONV_HIST, D_CONV),
                                 lambda bi, i: (bi, jnp.maximum(i * nb - 1, 0), 0))
    else:
        hist_spec = pl.BlockSpec((tb, CONV_HIST, D_CONV), lambda bi, i: (bi, 0, 0))
    row = lambda w: pl.BlockSpec((1, w), lambda bi, i: (0, 0))
    tile = lambda w: pl.BlockSpec((tb, t, w), lambda bi, i: (bi, i, 0))
    return pl.pallas_call(
        functools.partial(_mixer_out_body, tb=tb, t=t, zero_first_hist=hist_from_u, att_gain=att_gain),
        out_shape=jax.ShapeDtypeStruct(x.shape, F32),
        grid=(b // tb, tt // t),
        in_specs=[tile(d), tile(D_ATT), tile(D_CONV), hist_spec, row(LANES),
                  pl.BlockSpec((CONV_W, D_CONV), lambda bi, i: (0, 0)),
                  row(D_CONV), row(D_CONV), row(D_CONV),
                  pl.BlockSpec((d, d), lambda bi, i: (0, 0))],
        out_specs=tile(d),
        scratch_shapes=[pltpu.VMEM((tb, CONV_HIST + t, D_CONV), F32), pltpu.VMEM((tb * t, D_CONV), F32)],
        compiler_params=_params(("parallel", "arbitrary")),
        name="mixer_out", interpret=interpret,
    )(x, att, u, hist, gsub, cw, cb, lng, lnb, wout)


def _mem_kv_body(m_ref, g_ref, wk_ref, wv_ref, k_ref, v_ref):
    mn = _rmsnorm(m_ref[...], g_ref[...]).astype(BF16)
    k_ref[...] = jnp.dot(mn, wk_ref[...], preferred_element_type=F32)
    v_ref[...] = jnp.dot(mn, wv_ref[...], preferred_element_type=F32)


def _mem_kv(mem2d, g, wk, wv, *, tm, interpret=False):
    n, d = mem2d.shape
    tile = pl.BlockSpec((tm, d), lambda i: (i, 0))
    full = pl.BlockSpec((d, d), lambda i: (0, 0))
    return pl.pallas_call(
        _mem_kv_body,
        out_shape=[jax.ShapeDtypeStruct((n, d), F32)] * 2,
        grid=(n // tm,),
        in_specs=[tile, pl.BlockSpec((1, d), lambda i: (0, 0)), full, full],
        out_specs=[tile, tile],
        compiler_params=_params(("parallel",)),
        name="mem_kv", interpret=interpret,
    )(mem2d, g, wk, wv)


def _mem_attn_body(x_ref, mk_ref, mv_ref, g_ref, wq_ref, wo_ref, o_ref, *, tb, t):
    m = tb * t
    d = x_ref.shape[-1]
    dh = d // MEM_HEADS
    x = x_ref[...].reshape(m, d)
    hn = _rmsnorm(x, g_ref[...]).astype(BF16)
    q = (jnp.dot(hn, wq_ref[...], preferred_element_type=F32) * (dh ** -0.5)).astype(BF16)
    outs = []
    for b in range(tb):
        heads = []
        for h in range(MEM_HEADS):
            qh = q[b * t:(b + 1) * t, h * dh:(h + 1) * dh]
            kh = mk_ref[b, :, h * dh:(h + 1) * dh].astype(BF16)
            vh = mv_ref[b, :, h * dh:(h + 1) * dh].astype(BF16)
            s = lax.dot_general(qh, kh, _NT, preferred_element_type=F32)
            e = jnp.exp(s - jnp.max(s, axis=-1, keepdims=True))
            p = e / jnp.sum(e, axis=-1, keepdims=True)
            heads.append(jnp.dot(p.astype(BF16), vh, preferred_element_type=F32))
        outs.append(jnp.concatenate(heads, axis=1))
    o = jnp.concatenate(outs, axis=0) if tb > 1 else outs[0]
    y = jnp.dot(o.astype(BF16), wo_ref[...], preferred_element_type=F32) + x
    o_ref[...] = y.reshape(o_ref.shape)


def _mem_attn(x, mk, mv, g, wq, wo, *, tb, t, interpret=False):
    b, tt, d = x.shape
    mt = mk.shape[1]
    tile = pl.BlockSpec((tb, t, d), lambda bi, i: (bi, i, 0))
    mem = pl.BlockSpec((tb, mt, d), lambda bi, i: (bi, 0, 0))
    full = pl.BlockSpec((d, d), lambda bi, i: (0, 0))
    return pl.pallas_call(
        functools.partial(_mem_attn_body, tb=tb, t=t),
        out_shape=jax.ShapeDtypeStruct(x.shape, F32),
        grid=(b // tb, tt // t),
        in_specs=[tile, mem, mem, pl.BlockSpec((1, d), lambda bi, i: (0, 0)), full, full],
        out_specs=tile,
        compiler_params=_params(("parallel", "arbitrary")),
        name="mem_attn", interpret=interpret,
    )(x, mk, mv, g, wq, wo)


def _topk16(s, want_rank):
    r = s.shape[0]
    iota = lax.broadcasted_iota(I32, s.shape, 0)
    vals, idxs = [], []
    rank = jnp.full(s.shape, PEER_TOPK, I32) if want_rank else None
    for k in range(PEER_TOPK):
        m = jnp.max(s, axis=0, keepdims=True)
        idx = jnp.min(jnp.where(s == m, iota, r), axis=0, keepdims=True)
        hit = iota == idx
        if want_rank:
            rank = jnp.where(hit, k, rank)
        s = jnp.where(hit, -jnp.inf, s)
        vals.append(m)
        idxs.append(idx)
    return jnp.concatenate(vals, axis=0), jnp.concatenate(idxs, axis=0), rank


def _route_exact(s1, s2):
    sv1, _, r1 = _topk16(s1, True)
    sv2, _, r2 = _topk16(s2, True)
    cand = jnp.concatenate([sv1[k:k + 1] + sv2 for k in range(PEER_TOPK)], axis=0)
    tv, tp, _ = _topk16(cand, False)
    k1 = tp >> 4
    n1 = jnp.zeros(s1.shape, F32)
    for k in range(PEER_TOPK):
        n1 = n1 + jnp.where(r1 == k1[k:k + 1], 1.0, 0.0)
    z = jnp.sum(jnp.exp(tv - tv[0:1]), axis=0, keepdims=True)
    return n1, r2, z


def _topk16_distinct(s, want_rank):
    vals = []
    rank = jnp.full(s.shape, PEER_TOPK, I32) if want_rank else None
    for k in range(PEER_TOPK):
        m = jnp.max(s, axis=0, keepdims=True)
        hit = s == m
        if want_rank:
            rank = jnp.where(hit, k, rank)
        s = jnp.where(hit, -jnp.inf, s)
        vals.append(m)
    return vals, rank, s


_CAND_GROUPS = (
    ((0, 0, 0, 8),),
    ((0, 0, 8, 8),),
    ((1, 0, 0, 8),),
    ((2, 0, 0, 5), (4, 5, 0, 3)),
    ((3, 0, 0, 4), (5, 4, 0, 2), (6, 6, 0, 2)),
    ((7, 0, 0, 2),) + tuple((8 + i, 2 + i, 0, 1) for i in range(6)),
    ((14, 0, 0, 1), (15, 1, 0, 1)),
)


def _route_distinct(s1, s2):
    t = s1.shape[1]
    sv1, r1, _ = _topk16_distinct(s1, True)
    sv2, r2, _ = _topk16_distinct(s2, True)
    sv2_lo = jnp.concatenate(sv2[:8], axis=0)
    sv2_hi = jnp.concatenate(sv2[8:], axis=0)
    row = lax.broadcasted_iota(I32, (8, t), 0)
    groups = []
    for pieces in _CAND_GROUPS:
        val = jnp.full((8, t), -jnp.inf, F32)
        for k1, off, k2, nk in pieces:
            src = sv2_hi if k2 else sv2_lo
            if off:
                src = pltpu.roll(src, off, 0)
            val = jnp.where((row >= off) & (row < off + nk), sv1[k1] + src, val)
        groups.append(val)
    cand = jnp.concatenate(groups, axis=0)
    tv, _, left = _topk16_distinct(cand, False)
    won = jnp.where(left != cand, 1.0, 0.0)
    cnt = [jnp.zeros((1, t), F32) for _ in range(PEER_TOPK)]
    for gi, pieces in enumerate(_CAND_GROUPS):
        w = won[gi * 8:(gi + 1) * 8]
        for k1, off, _, nk in pieces:
            cnt[k1] = cnt[k1] + jnp.sum(jnp.where((row >= off) & (row < off + nk), w, 0.0),
                                        axis=0, keepdims=True)
    n1 = jnp.zeros(s1.shape, F32)
    for k in range(PEER_TOPK):
        n1 = jnp.where(r1 == k, cnt[k], n1)
    z = jnp.zeros((1, t), F32)
    for k in range(PEER_TOPK):
        z = z + jnp.exp(tv[k] - tv[0])
    picked = lambda r: jnp.sum(jnp.where(r < PEER_TOPK, 1.0, 0.0), axis=0, keepdims=True)
    total = cnt[0]
    for k in range(1, PEER_TOPK):
        total = total + cnt[k]
    tied = (picked(r1) != PEER_TOPK) | (picked(r2) != PEER_TOPK) | (total != PEER_TOPK)
    return n1, r2, z, sv1[0], sv2[0], tied


def _peer_route_body(x_ref, g_ref, wpq_ref, sk_ref, xn_ref, n1_ref, e1_ref, r2_ref, e2_ref, q_ref):
    xnt = _rmsnorm(x_ref[...], g_ref[...]).T.astype(BF16)
    xn_ref[...] = xnt
    q_ref[...] = jnp.dot(wpq_ref[...], xnt, preferred_element_type=F32).astype(BF16)

    def head(h, carry):
        def scores(c):
            r0 = pl.multiple_of((h * 2 + c) * LANES, LANES)
            return jnp.dot(sk_ref[h * 2 + c], q_ref[pl.ds(r0, LANES), :],
                           preferred_element_type=F32)
        s1 = scores(0)
        s2 = scores(1)
        n1, r2, z, top1, top2, tied = _route_distinct(s1, s2)
        ex2 = jnp.exp(s2 - top2)
        n1_ref[h] = n1
        e1_ref[h] = jnp.exp(s1 - top1)
        r2_ref[h] = r2.astype(BF16)
        e2_ref[h] = (ex2 / z).astype(BF16)

        @pl.when(jnp.max(jnp.where(tied, 1.0, 0.0)) > 0.5)
        def _():
            n1x, r2x, zx = _route_exact(s1, s2)
            n1_ref[h] = n1x
            r2_ref[h] = r2x.astype(BF16)
            e2_ref[h] = (ex2 / zx).astype(BF16)

        return carry

    lax.fori_loop(0, PEER_HEADS, head, 0)


def _peer_route(x2d, g, wpq, sk, *, tm, interpret=False):
    n, d = x2d.shape
    dq = wpq.shape[0]
    route = pl.BlockSpec((PEER_HEADS, N_KEYS, tm), lambda i: (0, 0, i))
    route_f32 = jax.ShapeDtypeStruct((PEER_HEADS, N_KEYS, n), F32)
    route_b16 = jax.ShapeDtypeStruct((PEER_HEADS, N_KEYS, n), BF16)
    return pl.pallas_call(
        _peer_route_body,
        out_shape=[jax.ShapeDtypeStruct((d, n), BF16), route_f32, route_f32, route_b16, route_b16],
        grid=(n // tm,),
        in_specs=[pl.BlockSpec((tm, d), lambda i: (i, 0)),
                  pl.BlockSpec((1, d), lambda i: (0, 0)),
                  pl.BlockSpec((dq, d), lambda i: (0, 0)),
                  pl.BlockSpec(sk.shape, lambda i: (0, 0, 0))],
        out_specs=[pl.BlockSpec((d, tm), lambda i: (0, i))] + [route] * 4,
        scratch_shapes=[pltpu.VMEM((dq, tm), BF16)],
        compiler_params=_params(("parallel",)),
        name="peer_route", interpret=interpret,
    )(x2d, g, wpq, sk)


def _row_bf16(row):
    slab = jnp.broadcast_to(row, (16, row.shape[1])).astype(BF16)
    return jnp.concatenate([slab] * (N_KEYS // 16), axis=0)


def _peer_dense_body(x_ref, xn_ref, n1_ref, e1_ref, r2_ref, e2_ref, u_ref, vt_ref, gf_ref, y_ref,
                     acc_ref, w_ref, *, ti):
    j = pl.program_id(1)

    @pl.when(j == 0)
    def _():
        acc_ref[...] = jnp.zeros(acc_ref.shape, F32)

    ht = jnp.dot(u_ref[...], xn_ref[...], preferred_element_type=F32)
    for il in range(ti):
        i1 = j * ti + il
        rows = slice(il * N_KEYS, (il + 1) * N_KEYS)
        gate = None
        for h in range(PEER_HEADS):
            n1 = _row_bf16(n1_ref[h, pl.ds(i1, 1), :])
            e1 = _row_bf16(e1_ref[h, pl.ds(i1, 1), :])
            term = jnp.where(r2_ref[h] < n1, e2_ref[h], jnp.zeros((), BF16)) * e1
            gate = term if gate is None else gate + term
        w_ref[rows, :] = _gelu(ht[rows, :]).astype(BF16) * gate
    acc_ref[...] += jnp.dot(vt_ref[...], w_ref[...], preferred_element_type=F32)

    @pl.when(j == pl.num_programs(1) - 1)
    def _():
        y_ref[...] = _rmsnorm(x_ref[...] + acc_ref[...].T, gf_ref[...])


def _peer_dense(x2d, xn, n1, e1, r2, e2, u_bf16, vt_bf16, g_final, *, tm, ti, interpret=False):
    n, d = x2d.shape
    ne = u_bf16.shape[0]
    te = ti * N_KEYS
    tok = lambda dt: pl.BlockSpec((tm, d), lambda i, j: (i, 0))
    route = pl.BlockSpec((PEER_HEADS, N_KEYS, tm), lambda i, j: (0, 0, i))
    return pl.pallas_call(
        functools.partial(_peer_dense_body, ti=ti),
        out_shape=jax.ShapeDtypeStruct((n, d), F32),
        grid=(n // tm, ne // te),
        in_specs=[tok(F32), pl.BlockSpec((d, tm), lambda i, j: (0, i)), route, route, route, route,
                  pl.BlockSpec((te, d), lambda i, j: (j, 0)),
                  pl.BlockSpec((d, te), lambda i, j: (0, j)),
                  pl.BlockSpec((1, d), lambda i, j: (0, 0))],
        out_specs=tok(F32),
        scratch_shapes=[pltpu.VMEM((d, tm), F32), pltpu.VMEM((te, tm), BF16)],
        compiler_params=_params(("parallel", "arbitrary")),
        name="peer_dense", interpret=interpret,
    )(x2d, xn, n1, e1, r2, e2, u_bf16, vt_bf16, g_final)


def _pick(n, pref):
    t = min(n, pref)
    while n % t:
        t //= 2
    return t


def _forward(x_prompt, x_sample, cache_k, cache_v, state_conv, cache_mem_k, cache_mem_v, page_table,
             mem_prompt, g_norm_mix, w_in, lam_q1, lam_k1, lam_q2, lam_k2, g_subln, conv_w, conv_b,
             conv_ln_g, conv_ln_b, w_out, g_norm_mem, g_mem_src, w_mq, w_mk, w_mv, w_mo, g_norm_peer,
             w_pq, sub_keys, u_tab, v_tab, g_final, *, interpret=False):
    depth = w_in.shape[0]
    assert depth == 1, "single layer: the final norm is fused into the PEER kernel"
    li = 0
    b, s, d = x_prompt.shape
    bd, ts, _ = x_sample.shape
    past = page_table.shape[1] * cache_k.shape[2]
    n_pool, page = cache_k.shape[1], cache_k.shape[2]
    mt = mem_prompt.shape[1]
    kw = dict(interpret=interpret)
    row = lambda a: a.reshape(1, -1)

    lam_init = 0.8 - 0.6 * math.exp(-0.3 * li)
    lams = (row(lam_q1[li]), row(lam_k1[li]), row(lam_q2[li]), row(lam_k2[li]))
    w_in_b = w_in[li].astype(BF16)
    mix_w = (row(g_subln[li]), conv_w[li], row(conv_b[li]), row(conv_ln_g[li]), row(conv_ln_b[li]),
             w_out[li].astype(BF16))

    tm_p = _pick(s, 512)
    w_nat = w_in_b[:, D_ATT:]
    w_t = jnp.concatenate([w_in_b[:, :D_ATT], w_in_b[:, 2 * D_ATT:3 * D_ATT]], axis=1).T
    qtp, kp, kbp, vp, vtp, up = _mixer_in_t(x_prompt.reshape(b * s, d), row(g_norm_mix[li]), w_nat, w_t,
                                            _rope_tables(jnp.arange(s, dtype=I32)), b=b, tm=tm_p, **kw)
    ns = bd * ts
    tm_s = _pick(ns, 512)
    cos_s, sin_s, _, _ = _rope_tables(past + jnp.arange(ts, dtype=I32))
    cos_s = jnp.tile(cos_s, (tm_s // ts, 1))
    sin_s = jnp.tile(sin_s, (tm_s // ts, 1))
    qs, ks, vs, us = _mixer_in(x_sample.reshape(ns, d), row(g_norm_mix[li]), w_in_b, cos_s, sin_s,
                               tm=tm_s, **kw)
    r3 = lambda a, bb: a.reshape(bb, -1, a.shape[-1])
    kp, kbp, vp, up = (r3(a, b) for a in (kp, kbp, vp, up))
    qs, ks, vs, us = (r3(a, bd) for a in (qs, ks, vs, us))

    att_p = _attn_prompt(lams, qtp, kbp, vtp, tq=_pick(s, 256), heads=4, lam_init=lam_init, **kw)
    flat = lambda c: c.reshape(c.shape[0], n_pool, page * ATT_HEADS, LANES)
    att_s = _attn_sample(page_table, lams, qs, ks.reshape(bd, ts * ATT_HEADS, LANES),
                         vs.reshape(bd, ts * ATT_HEADS, LANES), flat(cache_k), flat(cache_v), li,
                         pp=_pick(page_table.shape[1], 16), lam_init=lam_init, **kw)

    xp = _mixer_out(x_prompt, att_p, up, up, mix_w, tb=1, t=tm_p, hist_from_u=True,
                    att_gain=1.0 - lam_init, **kw)
    state = state_conv[li]
    hist_s = jnp.pad(state, ((0, 0), (CONV_HIST - state.shape[1], 0), (0, 0)))
    xs = _mixer_out(x_sample, att_s, us, hist_s, mix_w, tb=_pick(bd, 16), t=ts, hist_from_u=False,
                    att_gain=1.0 - lam_init, **kw)
    conv_p = up[:, s - (CONV_W - 1):]
    conv_s = jnp.concatenate([state, us], axis=1)[:, -(CONV_W - 1):]

    mk, mv = _mem_kv(mem_prompt.reshape(b * mt, d), row(g_mem_src[li]), w_mk[li].astype(BF16),
                     w_mv[li].astype(BF16), tm=_pick(b * mt, 512), **kw)
    mk = mk.reshape(b, mt, d)
    mv = mv.reshape(b, mt, d)
    wq_b, wo_b = w_mq[li].astype(BF16), w_mo[li].astype(BF16)
    xp = _mem_attn(xp, mk, mv, row(g_norm_mem[li]), wq_b, wo_b, tb=1, t=tm_p, **kw)
    xs = _mem_attn(xs, cache_mem_k[li].reshape(bd, mt, d), cache_mem_v[li].reshape(bd, mt, d),
                   row(g_norm_mem[li]), wq_b, wo_b, tb=_pick(bd, 8), t=ts, **kw)

    x_all = jnp.concatenate([xp.reshape(b * s, d), xs.reshape(ns, d)], axis=0)
    n_all = x_all.shape[0]
    sk = sub_keys[li].reshape(PEER_HEADS * 2, N_KEYS, -1).astype(BF16)
    xn, n1, e1, r2, e2 = _peer_route(x_all, row(g_norm_peer[li]), w_pq[li].T.astype(BF16), sk,
                                     tm=_pick(n_all, 256), **kw)
    y = _peer_dense(x_all, xn, n1, e1, r2, e2, u_tab[li].astype(BF16), v_tab[li].T.astype(BF16),
                    row(g_final), tm=_pick(n_all, 512), ti=_pick(N_KEYS, 8), **kw)
    y_prompt = y[:b * s].reshape(b, s, d)
    y_sample = y[b * s:].reshape(bd, ts, d)

    k4 = lambda a: a.reshape(1, a.shape[0], a.shape[1], ATT_HEADS, -1)
    return (y_prompt, y_sample, k4(kp), k4(vp), conv_p[None],
            mk.reshape(1, b, mt, MEM_HEADS, -1), mv.reshape(1, b, mt, MEM_HEADS, -1),
            k4(ks), k4(vs), conv_s[None])


def kernel(x_prompt, x_sample, cache_k, cache_v, state_conv, cache_mem_k, cache_mem_v, page_table, mem_prompt, g_norm_mix, w_in, lam_q1, lam_k1, lam_q2, lam_k2, g_subln, conv_w, conv_b, conv_ln_g, conv_ln_b, w_out, g_norm_mem, g_mem_src, w_mq, w_mk, w_mv, w_mo, g_norm_peer, w_pq, sub_keys, u_tab, v_tab, g_final):
    return _forward(x_prompt, x_sample, cache_k, cache_v, state_conv, cache_mem_k, cache_mem_v, page_table,
                    mem_prompt, g_norm_mix, w_in, lam_q1, lam_k1, lam_q2, lam_k2, g_subln, conv_w, conv_b,
                    conv_ln_g, conv_ln_b, w_out, g_norm_mem, g_mem_src, w_mq, w_mk, w_mv, w_mo, g_norm_peer,
                    w_pq, sub_keys, u_tab, v_tab, g_final)
```

```python
import functools
import math

import jax
import jax.numpy as jnp
from jax import lax
from jax.experimental import pallas as pl
from jax.experimental.pallas import tpu as pltpu

F32 = jnp.float32
BF16 = jnp.bfloat16
I32 = jnp.int32

EPS = 1e-6
ROPE_THETA = 10000.0
ATT_HEADS = 4
ATT_DH = 64
ATT_DV = 128
D_ATT = 512
D_CONV = 512
CONV_W = 31
CONV_HIST = 32
MEM_HEADS = 4
PEER_HEADS = 8
N_KEYS = 128
PEER_TOPK = 16
LANES = 128
VMEM_LIMIT = 56 * 1024 * 1024

_NT = (((1,), (1,)), ((), ()))


def _params(sem, vmem=VMEM_LIMIT):
    return pltpu.CompilerParams(dimension_semantics=sem, vmem_limit_bytes=vmem)


def _rmsnorm(x, g):
    return x * lax.rsqrt(jnp.mean(x * x, axis=-1, keepdims=True) + EPS) * g


def _gelu(x):
    return 0.5 * x * (1.0 + lax.erf(x * (2.0 ** -0.5)))


def _lam(lq1, lk1, lq2, lk2, lam_init):
    a = jnp.sum(lq1 * lk1, axis=-1, keepdims=True)
    b = jnp.sum(lq2 * lk2, axis=-1, keepdims=True)
    return jnp.exp(a) - jnp.exp(b) + lam_init


def _mixer_in_body(x_ref, g_ref, w_ref, cos_ref, sin_ref, q_ref, k_ref, v_ref, u_ref):
    xn = _rmsnorm(x_ref[...], g_ref[...])
    z = jnp.dot(xn.astype(BF16), w_ref[...], preferred_element_type=F32)
    cos = cos_ref[...]
    sin = sin_ref[...]
    lane = lax.broadcasted_iota(I32, (1, LANES), 1)
    first_half = (lane & 32) == 0

    def rope(t):
        partner = jnp.where(first_half, pltpu.roll(t, 96, 1), pltpu.roll(t, 32, 1))
        return t * cos + partner * sin

    for h in range(ATT_HEADS):
        lo = h * LANES
        q_ref[:, lo:lo + LANES] = rope(z[:, lo:lo + LANES])
        k_ref[:, lo:lo + LANES] = rope(z[:, D_ATT + lo:D_ATT + lo + LANES])
    v_ref[...] = z[:, 2 * D_ATT:3 * D_ATT]
    a = z[:, 3 * D_ATT:3 * D_ATT + D_CONV]
    gate = z[:, 3 * D_ATT + D_CONV:]
    u_ref[...] = a * jax.nn.sigmoid(gate)


def _mixer_in_t_body(x_ref, g_ref, w_ref, wt_ref, cos_ref, sin_ref, cost_ref, sint_ref,
                     qt_ref, k_ref, kb_ref, v_ref, vt_ref, u_ref):
    xn = _rmsnorm(x_ref[...], g_ref[...]).astype(BF16)
    z = jnp.dot(xn, w_ref[...], preferred_element_type=F32)
    zt = lax.dot_general(wt_ref[...], xn, _NT, preferred_element_type=F32)
    cos = cos_ref[...]
    sin = sin_ref[...]
    lane = lax.broadcasted_iota(I32, (1, LANES), 1)
    first_half = (lane & 32) == 0
    for h in range(ATT_HEADS):
        t = z[:, h * LANES:(h + 1) * LANES]
        partner = jnp.where(first_half, pltpu.roll(t, 96, 1), pltpu.roll(t, 32, 1))
        kh = t * cos + partner * sin
        k_ref[:, h * LANES:(h + 1) * LANES] = kh
        kb_ref[:, h * LANES:(h + 1) * LANES] = kh.astype(BF16)
    v_ref[...] = z[:, D_ATT:2 * D_ATT]
    u_ref[...] = z[:, 2 * D_ATT:2 * D_ATT + D_CONV] * jax.nn.sigmoid(z[:, 2 * D_ATT + D_CONV:])
    ct = cost_ref[...]
    st = sint_ref[...]
    half = ATT_DH // 2
    qscale = (ATT_DH ** -0.5) * math.log2(math.e)
    for blk in range(2 * ATT_HEADS):
        r0 = blk * ATT_DH
        x1 = zt[r0:r0 + half]
        x2 = zt[r0 + half:r0 + ATT_DH]
        qt_ref[r0:r0 + half, :] = ((x1 * ct - x2 * st) * qscale).astype(BF16)
        qt_ref[r0 + half:r0 + ATT_DH, :] = ((x2 * ct + x1 * st) * qscale).astype(BF16)
    vt_ref[...] = zt[D_ATT:].astype(BF16)


def _mixer_in_t(x2d, g, w_bf16, wt_bf16, tabs, *, b, tm, interpret=False):
    n, d = x2d.shape
    s = n // b
    nt = s // tm
    cos, sin, cost, sint = tabs
    tab = pl.BlockSpec((tm, LANES), lambda i: (i % nt, 0))
    tabt = pl.BlockSpec((ATT_DH // 2, tm), lambda i: (0, i % nt))
    nat = pl.BlockSpec((tm, D_ATT), lambda i: (i, 0))
    fmaj = pl.BlockSpec((None, D_ATT, tm), lambda i: (i // nt, 0, i % nt))
    nat_f32 = jax.ShapeDtypeStruct((n, D_ATT), F32)
    fmaj_b16 = jax.ShapeDtypeStruct((b, D_ATT, s), BF16)
    return pl.pallas_call(
        _mixer_in_t_body,
        out_shape=[fmaj_b16, nat_f32, jax.ShapeDtypeStruct((n, D_ATT), BF16), nat_f32, fmaj_b16, nat_f32],
        grid=(n // tm,),
        in_specs=[pl.BlockSpec((tm, d), lambda i: (i, 0)),
                  pl.BlockSpec((1, d), lambda i: (0, 0)),
                  pl.BlockSpec(w_bf16.shape, lambda i: (0, 0)),
                  pl.BlockSpec(wt_bf16.shape, lambda i: (0, 0)),
                  tab, tab, tabt, tabt],
        out_specs=[fmaj, nat, nat, nat, fmaj, nat],
        compiler_params=_params(("parallel",)),
        name="mixer_in_t", interpret=interpret,
    )(x2d, g, w_bf16, wt_bf16, cos, sin, cost, sint)


def _rope_tables(pos):
    inv = ROPE_THETA ** (-jnp.arange(0, ATT_DH, 2, dtype=F32) / ATT_DH)
    ang = pos.astype(F32)[:, None] * inv[None, :]
    cos = jnp.tile(jnp.cos(ang), (1, 4))
    sin = jnp.sin(ang)
    sin4 = jnp.concatenate([-sin, sin, -sin, sin], axis=1)
    return cos, sin4, jnp.cos(ang).T, sin.T


def _mixer_in(x2d, g, w_bf16, cos, sin, *, tm, interpret=False):
    n, d = x2d.shape
    d_in = w_bf16.shape[1]
    nt = cos.shape[0] // tm
    tab = pl.BlockSpec((tm, LANES), lambda i: (i % nt, 0))
    out = pl.BlockSpec((tm, D_ATT), lambda i: (i, 0))
    return pl.pallas_call(
        _mixer_in_body,
        out_shape=[jax.ShapeDtypeStruct((n, D_ATT), F32)] * 4,
        grid=(n // tm,),
        in_specs=[pl.BlockSpec((tm, d), lambda i: (i, 0)),
                  pl.BlockSpec((1, d), lambda i: (0, 0)),
                  pl.BlockSpec((d, d_in), lambda i: (0, 0)),
                  tab, tab],
        out_specs=[out] * 4,
        compiler_params=_params(("parallel",)),
        name="mixer_in", interpret=interpret,
    )(x2d, g, w_bf16, cos, sin)


def _attn_prompt_body(lq1, lk1, lq2, lk2, qt_ref, k_ref, vt_ref, o_ref, *, tq, lam_init):
    qi = pl.program_id(2)
    lam = _lam(lq1[...], lk1[...], lq2[...], lk2[...], lam_init)
    heads = qt_ref.shape[0] // LANES
    feat = lax.broadcasted_iota(I32, (LANES, tq), 0)
    zero = jnp.zeros((), BF16)
    qq = []
    for g in range(heads):
        qt = qt_ref[g * LANES:(g + 1) * LANES, :]
        qq.append(jnp.concatenate([jnp.where(feat < ATT_DH, qt, zero),
                                   jnp.where(feat >= ATT_DH, qt, zero)], axis=1))

    def step(j, carry, masked):
        start = pl.multiple_of(j * tq, tq)
        out = []
        scores = [jnp.dot(k_ref[pl.ds(start, tq), g * LANES:(g + 1) * LANES], qq[g],
                          preferred_element_type=F32) for g in range(heads)]
        for g in range(heads):
            m, l, acc = carry[g]
            vj = vt_ref[g * LANES:(g + 1) * LANES, pl.ds(start, tq)]
            s = scores[g]
            if masked:
                kpos = lax.broadcasted_iota(I32, s.shape, 0)
                qpos = lax.broadcasted_iota(I32, s.shape, 1) & (tq - 1)
                s = jnp.where(kpos <= qpos, s, -jnp.inf)
            m_new = jnp.maximum(m, jnp.max(s, axis=0, keepdims=True))
            alpha = jnp.exp2(m - m_new)
            p = jnp.exp2(s - m_new)
            l = l * alpha + jnp.sum(p, axis=0, keepdims=True)
            acc = acc * alpha + jnp.dot(vj, p.astype(BF16), preferred_element_type=F32)
            out.append((m_new, l, acc))
        return tuple(out)

    init = tuple((jnp.full((1, 2 * tq), -jnp.inf, F32), jnp.zeros((1, 2 * tq), F32),
                  jnp.zeros((ATT_DV, 2 * tq), F32)) for _ in range(heads))
    carry = lax.fori_loop(0, qi, lambda j, c: step(j, c, False), init)
    carry = step(qi, carry, True)
    for g in range(heads):
        _, l, acc = carry[g]
        o = acc / l
        o_ref[:, g * LANES:(g + 1) * LANES] = (o[:, :tq] - lam * o[:, tq:]).T


def _attn_prompt(lams, qt, kb, vt, *, tq, heads, lam_init, interpret=False):
    b, s, _ = kb.shape
    assert tq & (tq - 1) == 0 and ATT_HEADS % heads == 0
    lam_spec = pl.BlockSpec((1, ATT_DH), lambda bi, h, i: (0, 0))
    w = heads * LANES
    return pl.pallas_call(
        functools.partial(_attn_prompt_body, tq=tq, lam_init=lam_init),
        out_shape=jax.ShapeDtypeStruct((b, s, D_ATT), F32),
        grid=(b, ATT_HEADS // heads, s // tq),
        in_specs=[lam_spec] * 4 + [pl.BlockSpec((None, w, tq), lambda bi, h, i: (bi, h, i)),
                                   pl.BlockSpec((None, s, w), lambda bi, h, i: (bi, 0, h)),
                                   pl.BlockSpec((None, w, s), lambda bi, h, i: (bi, h, 0))],
        out_specs=pl.BlockSpec((None, tq, w), lambda bi, h, i: (bi, i, h)),
        compiler_params=_params(("parallel", "parallel", "arbitrary")),
        name="attn_prompt", interpret=interpret,
    )(*lams, qt, kb, vt)


def _attn_sample_body(pt_ref, lq1, lk1, lq2, lk2, q_ref, kn_ref, vn_ref, *rest, pp, t, lam_init):
    k_refs = rest[:pp]
    v_refs = rest[pp:2 * pp]
    o_ref = rest[2 * pp]
    qm_ref, m_ref, l_ref, acc_ref = rest[2 * pp + 1:]
    g = pl.program_id(1)
    rows = ATT_HEADS * 2 * t
    head_shift = (2 * t).bit_length() - 1

    @pl.when(g == 0)
    def _():
        q = q_ref[...] * (ATT_DH ** -0.5)
        lane = lax.broadcasted_iota(I32, (t, LANES), 1)
        blocks = []
        for h in range(ATT_HEADS):
            qh = q[:, h * LANES:(h + 1) * LANES]
            blocks.append(jnp.where(lane < ATT_DH, qh, 0.0))
            blocks.append(jnp.where(lane >= ATT_DH, qh, 0.0))
        qm_ref[...] = jnp.concatenate(blocks, axis=0).astype(BF16)
        m_ref[...] = jnp.full(m_ref.shape, -jnp.inf, F32)
        l_ref[...] = jnp.zeros(l_ref.shape, F32)
        acc_ref[...] = jnp.zeros(acc_ref.shape, F32)

    def online(s_list, v_list):
        m_prev = m_ref[...]
        m_new = m_prev
        for s in s_list:
            m_new = jnp.maximum(m_new, jnp.max(s, axis=-1, keepdims=True))
        alpha = jnp.exp(m_prev - m_new)
        l = l_ref[...] * alpha
        acc = acc_ref[...] * alpha
        for s, v in zip(s_list, v_list):
            p = jnp.exp(s - m_new)
            l = l + jnp.sum(p, axis=-1, keepdims=True)
            acc = acc + jnp.dot(p.astype(BF16), v, preferred_element_type=F32)
        l_ref[...] = l
        acc_ref[...] = acc
        m_ref[...] = m_new

    qm = qm_ref[...]
    nrow = k_refs[0].shape[0]
    same_head = ((lax.broadcasted_iota(I32, (rows, nrow), 0) >> head_shift)
                 == (lax.broadcasted_iota(I32, (rows, nrow), 1) & (ATT_HEADS - 1)))
    online([jnp.where(same_head,
                      lax.dot_general(qm, k_refs[j][...].astype(BF16), _NT, preferred_element_type=F32),
                      -jnp.inf) for j in range(pp)],
           [v_refs[j][...].astype(BF16) for j in range(pp)])

    @pl.when(g == pl.num_programs(1) - 1)
    def _():
        s2 = lax.dot_general(qm, kn_ref[...].astype(BF16), _NT, preferred_element_type=F32)
        r = lax.broadcasted_iota(I32, s2.shape, 0)
        c = lax.broadcasted_iota(I32, s2.shape, 1)
        ok = ((r >> head_shift) == (c & (ATT_HEADS - 1))) & ((c >> 2) <= (r & (t - 1)))
        online([jnp.where(ok, s2, -jnp.inf)], [vn_ref[...].astype(BF16)])
        lam = _lam(lq1[...], lk1[...], lq2[...], lk2[...], lam_init)
        o = acc_ref[...] / l_ref[...]
        for h in range(ATT_HEADS):
            oh = o[h * 2 * t:(h + 1) * 2 * t]
            o_ref[:, h * LANES:(h + 1) * LANES] = oh[:t] - lam * oh[t:]


def _attn_sample(page_table, lams, q, k_new, v_new, cache_k, cache_v, li, *, pp, lam_init, interpret=False):
    bd, t, _ = q.shape
    n_pages = page_table.shape[1]
    prow = cache_k.shape[2]
    assert n_pages % pp == 0 and t & (t - 1) == 0 and ATT_HEADS == 4
    rows = ATT_HEADS * 2 * t
    lam_spec = pl.BlockSpec((1, ATT_DH), lambda b, g, pt: (0, 0))
    tok_spec = pl.BlockSpec((None, t, D_ATT), lambda b, g, pt: (b, 0, 0))
    new_spec = pl.BlockSpec((None, t * ATT_HEADS, LANES), lambda b, g, pt: (b, 0, 0))

    def page_spec(j):
        return pl.BlockSpec((None, None, prow, LANES), lambda b, g, pt: (li, pt[b, g * pp + j], 0, 0))

    grid_spec = pltpu.PrefetchScalarGridSpec(
        num_scalar_prefetch=1,
        grid=(bd, n_pages // pp),
        in_specs=[lam_spec] * 4 + [tok_spec, new_spec, new_spec] + [page_spec(j) for j in range(pp)] * 2,
        out_specs=tok_spec,
        scratch_shapes=[pltpu.VMEM((rows, LANES), BF16), pltpu.VMEM((rows, 1), F32),
                        pltpu.VMEM((rows, 1), F32), pltpu.VMEM((rows, LANES), F32)],
    )
    return pl.pallas_call(
        functools.partial(_attn_sample_body, pp=pp, t=t, lam_init=lam_init),
        out_shape=jax.ShapeDtypeStruct((bd, t, D_ATT), F32),
        grid_spec=grid_spec,
        compiler_params=_params(("parallel", "arbitrary")),
        name="attn_sample", interpret=interpret,
    )(page_table, *lams, q, k_new, v_new, *([cache_k] * pp), *([cache_v] * pp))


def _mixer_out_body(x_ref, att_ref, u_ref, hist_ref, gsub_ref, cw_ref, cb_ref, lng_ref, lnb_ref,
                    wout_ref, o_ref, ucat_ref, ush_ref, conv_ref, *, tb, t, zero_first_hist, att_gain):
    hist = hist_ref[...]
    if zero_first_hist:
        hist = jnp.where(pl.program_id(1) == 0, 0.0, hist)
    ucat_ref[:, 0:CONV_HIST, :] = hist
    ucat_ref[:, CONV_HIST:CONV_HIST + t, :] = u_ref[...]
    rc = min(t, 64)
    off = CONV_HIST - (CONV_W - 1)
    span = t + CONV_HIST - 8
    for r in range(1, 8):
        ush_ref[r - 1, :, 0:span, :] = ucat_ref[:, r:r + span, :]
    for b in range(tb):
        for r0 in range(0, t, rc):
            acc = jnp.broadcast_to(cb_ref[...], (rc, D_CONV))
            for j in range(CONV_W):
                r = (j + off) % 8
                base = r0 + j + off - r
                src = ucat_ref[b, base:base + rc, :] if r == 0 else ush_ref[r - 1, b, base:base + rc, :]
                acc = acc + cw_ref[j:j + 1, :] * src
            conv_ref[b * t + r0:b * t + r0 + rc, :] = acc
    c = conv_ref[...]
    mu = jnp.mean(c, axis=-1, keepdims=True)
    cc = c - mu
    var = jnp.mean(cc * cc, axis=-1, keepdims=True)
    cn = cc * lax.rsqrt(var + EPS) * lng_ref[...] + lnb_ref[...]
    cact = cn * jax.nn.sigmoid(cn)
    m = tb * t
    att = att_ref[...].reshape(m, D_ATT)
    gsub = gsub_ref[...] * att_gain
    a = jnp.concatenate(
        [_rmsnorm(att[:, h * LANES:(h + 1) * LANES], gsub) for h in range(ATT_HEADS)], axis=1)
    y = jnp.dot(a.astype(BF16), wout_ref[0:D_ATT, :], preferred_element_type=F32)
    y = y + jnp.dot(cact.astype(BF16), wout_ref[D_ATT:, :], preferred_element_type=F32)
    o_ref[...] = (y + x_ref[...].reshape(m, -1)).reshape(o_ref.shape)


def _mixer_out(x, att, u, hist, weights, *, tb, t, hist_from_u, att_gain, interpret=False):
    b, tt, d = x.shape
    gsub, cw, cb, lng, lnb, wout = weights
    nb = t // CONV_HIST
    if hist_from_u:
        hist_spec = pl.BlockSpec((tb, CONV_HIST, D_CONV),
                                 lambda bi, i: (bi, jnp.maximum(i * nb - 1, 0), 0))
    else:
        hist_spec = pl.BlockSpec((tb, CONV_HIST, D_CONV), lambda bi, i: (bi, 0, 0))
    row = lambda w: pl.BlockSpec((1, w), lambda bi, i: (0, 0))
    tile = lambda w: pl.BlockSpec((tb, t, w), lambda bi, i: (bi, i, 0))
    return pl.pallas_call(
        functools.partial(_mixer_out_body, tb=tb, t=t, zero_first_hist=hist_from_u, att_gain=att_gain),
        out_shape=jax.ShapeDtypeStruct(x.shape, F32),
        grid=(b // tb, tt // t),
        in_specs=[tile(d), tile(D_ATT), tile(D_CONV), hist_spec, row(LANES),
                  pl.BlockSpec((CONV_W, D_CONV), lambda bi, i: (0, 0)),
                  row(D_CONV), row(D_CONV), row(D_CONV),
                  pl.BlockSpec((d, d), lambda bi, i: (0, 0))],
        out_specs=tile(d),
        scratch_shapes=[pltpu.VMEM((tb, CONV_HIST + t, D_CONV), F32),
                        pltpu.VMEM((7, tb, CONV_HIST + t - 8, D_CONV), F32),
                        pltpu.VMEM((tb * t, D_CONV), F32)],
        compiler_params=_params(("parallel", "arbitrary")),
        name="mixer_out", interpret=interpret,
    )(x, att, u, hist, gsub, cw, cb, lng, lnb, wout)


def _mem_kv_body(m_ref, g_ref, wk_ref, wv_ref, k_ref, v_ref):
    mn = _rmsnorm(m_ref[...], g_ref[...]).astype(BF16)
    k_ref[...] = jnp.dot(mn, wk_ref[...], preferred_element_type=F32)
    v_ref[...] = jnp.dot(mn, wv_ref[...], preferred_element_type=F32)


def _mem_kv(mem2d, g, wk, wv, *, tm, interpret=False):
    n, d = mem2d.shape
    tile = pl.BlockSpec((tm, d), lambda i: (i, 0))
    full = pl.BlockSpec((d, d), lambda i: (0, 0))
    return pl.pallas_call(
        _mem_kv_body,
        out_shape=[jax.ShapeDtypeStruct((n, d), F32)] * 2,
        grid=(n // tm,),
        in_specs=[tile, pl.BlockSpec((1, d), lambda i: (0, 0)), full, full],
        out_specs=[tile, tile],
        compiler_params=_params(("parallel",)),
        name="mem_kv", interpret=interpret,
    )(mem2d, g, wk, wv)


def _mem_attn_body(x_ref, mk_ref, mv_ref, g_ref, wq_ref, wo_ref, o_ref, *, tb, t):
    m = tb * t
    d = x_ref.shape[-1]
    dh = d // MEM_HEADS
    x = x_ref[...].reshape(m, d)
    hn = _rmsnorm(x, g_ref[...]).astype(BF16)
    q = (jnp.dot(hn, wq_ref[...], preferred_element_type=F32) * (dh ** -0.5)).astype(BF16)
    outs = []
    for b in range(tb):
        heads = []
        for h in range(MEM_HEADS):
            qh = q[b * t:(b + 1) * t, h * dh:(h + 1) * dh]
            if len(mk_ref.shape) == 4:
                kh = mk_ref[b, :, h, :].astype(BF16)
                vh = mv_ref[b, :, h, :].astype(BF16)
            else:
                kh = mk_ref[b, :, h * dh:(h + 1) * dh].astype(BF16)
                vh = mv_ref[b, :, h * dh:(h + 1) * dh].astype(BF16)
            s = lax.dot_general(qh, kh, _NT, preferred_element_type=F32)
            e = jnp.exp(s - jnp.max(s, axis=-1, keepdims=True))
            p = e / jnp.sum(e, axis=-1, keepdims=True)
            heads.append(jnp.dot(p.astype(BF16), vh, preferred_element_type=F32))
        outs.append(jnp.concatenate(heads, axis=1))
    o = jnp.concatenate(outs, axis=0) if tb > 1 else outs[0]
    y = jnp.dot(o.astype(BF16), wo_ref[...], preferred_element_type=F32) + x
    o_ref[...] = y.reshape(o_ref.shape)


def _mem_attn(x, mk, mv, g, wq, wo, *, tb, t, interpret=False):
    b, tt, d = x.shape
    mt = mk.shape[1]
    tile = pl.BlockSpec((tb, t, d), lambda bi, i: (bi, i, 0))
    if mk.ndim == 4:
        mem = pl.BlockSpec((tb, mt) + mk.shape[2:], lambda bi, i: (bi, 0, 0, 0))
    else:
        mem = pl.BlockSpec((tb, mt, d), lambda bi, i: (bi, 0, 0))
    full = pl.BlockSpec((d, d), lambda bi, i: (0, 0))
    return pl.pallas_call(
        functools.partial(_mem_attn_body, tb=tb, t=t),
        out_shape=jax.ShapeDtypeStruct(x.shape, F32),
        grid=(b // tb, tt // t),
        in_specs=[tile, mem, mem, pl.BlockSpec((1, d), lambda bi, i: (0, 0)), full, full],
        out_specs=tile,
        compiler_params=_params(("parallel", "arbitrary")),
        name="mem_attn", interpret=interpret,
    )(x, mk, mv, g, wq, wo)


def _topk16(s, want_rank):
    r = s.shape[0]
    iota = lax.broadcasted_iota(I32, s.shape, 0)
    vals, idxs = [], []
    rank = jnp.full(s.shape, PEER_TOPK, I32) if want_rank else None
    for k in range(PEER_TOPK):
        m = jnp.max(s, axis=0, keepdims=True)
        idx = jnp.min(jnp.where(s == m, iota, r), axis=0, keepdims=True)
        hit = iota == idx
        if want_rank:
            rank = jnp.where(hit, k, rank)
        s = jnp.where(hit, -jnp.inf, s)
        vals.append(m)
        idxs.append(idx)
    return jnp.concatenate(vals, axis=0), jnp.concatenate(idxs, axis=0), rank


def _route_exact(s1, s2):
    sv1, _, r1 = _topk16(s1, True)
    sv2, _, r2 = _topk16(s2, True)
    cand = jnp.concatenate([sv1[k:k + 1] + sv2 for k in range(PEER_TOPK)], axis=0)
    tv, tp, _ = _topk16(cand, False)
    k1 = tp >> 4
    n1 = jnp.zeros(s1.shape, F32)
    for k in range(PEER_TOPK):
        n1 = n1 + jnp.where(r1 == k1[k:k + 1], 1.0, 0.0)
    z = jnp.sum(jnp.exp(tv - tv[0:1]), axis=0, keepdims=True)
    return n1, r2, z


def _topk16_distinct(s, want_rank):
    vals = []
    rank = jnp.full(s.shape, PEER_TOPK, I32) if want_rank else None
    for k in range(PEER_TOPK):
        m = jnp.max(s, axis=0, keepdims=True)
        hit = s == m
        if want_rank:
            rank = jnp.where(hit, k, rank)
        s = jnp.where(hit, -jnp.inf, s)
        vals.append(m)
    return vals, rank, s


_CAND_GROUPS = (
    ((0, 0, 0, 8),),
    ((0, 0, 8, 8),),
    ((1, 0, 0, 8),),
    ((2, 0, 0, 5), (4, 5, 0, 3)),
    ((3, 0, 0, 4), (5, 4, 0, 2), (6, 6, 0, 2)),
    ((7, 0, 0, 2),) + tuple((8 + i, 2 + i, 0, 1) for i in range(6)),
    ((14, 0, 0, 1), (15, 1, 0, 1)),
)


def _route_distinct(s1, s2):
    t = s1.shape[1]
    sv1, r1, _ = _topk16_distinct(s1, True)
    sv2, r2, _ = _topk16_distinct(s2, True)
    sv2_lo = jnp.concatenate(sv2[:8], axis=0)
    sv2_hi = jnp.concatenate(sv2[8:], axis=0)
    row = lax.broadcasted_iota(I32, (8, t), 0)
    groups = []
    for pieces in _CAND_GROUPS:
        val = jnp.full((8, t), -jnp.inf, F32)
        for k1, off, k2, nk in pieces:
            src = sv2_hi if k2 else sv2_lo
            if off:
                src = pltpu.roll(src, off, 0)
            val = jnp.where((row >= off) & (row < off + nk), sv1[k1] + src, val)
        groups.append(val)
    cand = jnp.concatenate(groups, axis=0)
    tv, _, left = _topk16_distinct(cand, False)
    won = jnp.where(left != cand, 1.0, 0.0)
    cnt = [jnp.zeros((1, t), F32) for _ in range(PEER_TOPK)]
    for gi, pieces in enumerate(_CAND_GROUPS):
        w = won[gi * 8:(gi + 1) * 8]
        for k1, off, _, nk in pieces:
            cnt[k1] = cnt[k1] + jnp.sum(jnp.where((row >= off) & (row < off + nk), w, 0.0),
                                        axis=0, keepdims=True)
    n1 = jnp.zeros(s1.shape, F32)
    for k in range(PEER_TOPK):
        n1 = jnp.where(r1 == k, cnt[k], n1)
    z = jnp.zeros((1, t), F32)
    for k in range(PEER_TOPK):
        z = z + jnp.exp(tv[k] - tv[0])
    picked = lambda r: jnp.sum(jnp.where(r < PEER_TOPK, 1.0, 0.0), axis=0, keepdims=True)
    total = cnt[0]
    for k in range(1, PEER_TOPK):
        total = total + cnt[k]
    tied = (picked(r1) != PEER_TOPK) | (picked(r2) != PEER_TOPK) | (total != PEER_TOPK)
    return n1, r2, z, sv1[0], sv2[0], tied


def _peer_route_body(x_ref, g_ref, wpq_ref, sk_ref, xn_ref, n1_ref, e1_ref, r2_ref, e2_ref, q_ref):
    xnt = _rmsnorm(x_ref[...], g_ref[...]).T.astype(BF16)
    xn_ref[...] = xnt
    q_ref[...] = jnp.dot(wpq_ref[...], xnt, preferred_element_type=F32).astype(BF16)

    def head(h, carry):
        def scores(c):
            r0 = pl.multiple_of((h * 2 + c) * LANES, LANES)
            return jnp.dot(sk_ref[h * 2 + c], q_ref[pl.ds(r0, LANES), :],
                           preferred_element_type=F32)
        s1 = scores(0)
        s2 = scores(1)
        n1, r2, z, top1, top2, tied = _route_distinct(s1, s2)
        ex2 = jnp.exp(s2 - top2)
        n1_ref[h] = n1
        e1_ref[h] = jnp.exp(s1 - top1)
        r2_ref[h] = r2.astype(BF16)
        e2_ref[h] = (ex2 / z).astype(BF16)

        @pl.when(jnp.max(jnp.where(tied, 1.0, 0.0)) > 0.5)
        def _():
            n1x, r2x, zx = _route_exact(s1, s2)
            n1_ref[h] = n1x
            r2_ref[h] = r2x.astype(BF16)
            e2_ref[h] = (ex2 / zx).astype(BF16)

        return carry

    lax.fori_loop(0, PEER_HEADS, head, 0)


def _peer_route(x2d, g, wpq, sk, *, tm, interpret=False):
    n, d = x2d.shape
    dq = wpq.shape[0]
    route = pl.BlockSpec((PEER_HEADS, N_KEYS, tm), lambda i: (0, 0, i))
    route_f32 = jax.ShapeDtypeStruct((PEER_HEADS, N_KEYS, n), F32)
    route_b16 = jax.ShapeDtypeStruct((PEER_HEADS, N_KEYS, n), BF16)
    return pl.pallas_call(
        _peer_route_body,
        out_shape=[jax.ShapeDtypeStruct((d, n), BF16), route_f32, route_f32, route_b16, route_b16],
        grid=(n // tm,),
        in_specs=[pl.BlockSpec((tm, d), lambda i: (i, 0)),
                  pl.BlockSpec((1, d), lambda i: (0, 0)),
                  pl.BlockSpec((dq, d), lambda i: (0, 0)),
                  pl.BlockSpec(sk.shape, lambda i: (0, 0, 0))],
        out_specs=[pl.BlockSpec((d, tm), lambda i: (0, i))] + [route] * 4,
        scratch_shapes=[pltpu.VMEM((dq, tm), BF16)],
        compiler_params=_params(("parallel",)),
        name="peer_route", interpret=interpret,
    )(x2d, g, wpq, sk)


def _row_bf16(row):
    slab = jnp.broadcast_to(row, (16, row.shape[1])).astype(BF16)
    return jnp.concatenate([slab] * (N_KEYS // 16), axis=0)


def _peer_dense_body(x_ref, xn_ref, n1_ref, e1_ref, r2_ref, e2_ref, u_ref, vt_ref, gf_ref, y_ref,
                     acc_ref, w_ref, *, ti):
    j = pl.program_id(1)

    @pl.when(j == 0)
    def _():
        acc_ref[...] = jnp.zeros(acc_ref.shape, F32)

    xn = xn_ref[...]
    ht = [jnp.dot(u_ref[il * N_KEYS:(il + 1) * N_KEYS, :], xn, preferred_element_type=F32)
          for il in range(ti)]
    out = None
    kc = 2 * N_KEYS
    assert ti % 8 == 0
    for il in range(ti):
        slab = pl.ds(pl.multiple_of(j * ti + (il // 8) * 8, 8), 8)
        sub = il % 8
        rows = slice(il * N_KEYS, (il + 1) * N_KEYS)
        gate = None
        for h in range(PEER_HEADS):
            n1 = _row_bf16(n1_ref[h, slab, :][sub:sub + 1])
            e1 = _row_bf16(e1_ref[h, slab, :][sub:sub + 1])
            term = jnp.where(r2_ref[h] < n1, e2_ref[h], jnp.zeros((), BF16)) * e1
            gate = term if gate is None else gate + term
        w_ref[rows, :] = _gelu(ht[il]).astype(BF16) * gate
        if (il + 1) * N_KEYS % kc == 0:
            k0 = (il + 1) * N_KEYS - kc
            part = jnp.dot(vt_ref[:, k0:k0 + kc], w_ref[k0:k0 + kc, :], preferred_element_type=F32)
            out = part if out is None else out + part
    acc_ref[...] += out

    @pl.when(j == pl.num_programs(1) - 1)
    def _():
        y_ref[...] = _rmsnorm(x_ref[...] + acc_ref[...].T, gf_ref[...])


def _peer_dense(x2d, xn, n1, e1, r2, e2, u_bf16, vt_bf16, g_final, *, tm, ti, interpret=False):
    n, d = x2d.shape
    ne = u_bf16.shape[0]
    te = ti * N_KEYS
    tok = lambda dt: pl.BlockSpec((tm, d), lambda i, j: (i, 0))
    route = pl.BlockSpec((PEER_HEADS, N_KEYS, tm), lambda i, j: (0, 0, i))
    return pl.pallas_call(
        functools.partial(_peer_dense_body, ti=ti),
        out_shape=jax.ShapeDtypeStruct((n, d), F32),
        grid=(n // tm, ne // te),
        in_specs=[tok(F32), pl.BlockSpec((d, tm), lambda i, j: (0, i)), route, route, route, route,
                  pl.BlockSpec((te, d), lambda i, j: (j, 0)),
                  pl.BlockSpec((d, te), lambda i, j: (0, j)),
                  pl.BlockSpec((1, d), lambda i, j: (0, 0))],
        out_specs=tok(F32),
        scratch_shapes=[pltpu.VMEM((d, tm), F32), pltpu.VMEM((te, tm), BF16)],
        compiler_params=_params(("parallel", "arbitrary")),
        name="peer_dense", interpret=interpret,
    )(x2d, xn, n1, e1, r2, e2, u_bf16, vt_bf16, g_final)


def _pick(n, pref):
    t = min(n, pref)
    while n % t:
        t //= 2
    return t


def _forward(x_prompt, x_sample, cache_k, cache_v, state_conv, cache_mem_k, cache_mem_v, page_table,
             mem_prompt, g_norm_mix, w_in, lam_q1, lam_k1, lam_q2, lam_k2, g_subln, conv_w, conv_b,
             conv_ln_g, conv_ln_b, w_out, g_norm_mem, g_mem_src, w_mq, w_mk, w_mv, w_mo, g_norm_peer,
             w_pq, sub_keys, u_tab, v_tab, g_final, *, interpret=False):
    depth = w_in.shape[0]
    assert depth == 1, "single layer: the final norm is fused into the PEER kernel"
    li = 0
    b, s, d = x_prompt.shape
    bd, ts, _ = x_sample.shape
    past = page_table.shape[1] * cache_k.shape[2]
    n_pool, page = cache_k.shape[1], cache_k.shape[2]
    mt = mem_prompt.shape[1]
    kw = dict(interpret=interpret)
    row = lambda a: a.reshape(1, -1)

    lam_init = 0.8 - 0.6 * math.exp(-0.3 * li)
    lams = (row(lam_q1[li]), row(lam_k1[li]), row(lam_q2[li]), row(lam_k2[li]))
    w_in_b = w_in[li].astype(BF16)
    mix_w = (row(g_subln[li]), conv_w[li], row(conv_b[li]), row(conv_ln_g[li]), row(conv_ln_b[li]),
             w_out[li].astype(BF16))

    tm_p = _pick(s, 512)
    w_nat = w_in_b[:, D_ATT:]
    w_t = jnp.concatenate([w_in_b[:, :D_ATT], w_in_b[:, 2 * D_ATT:3 * D_ATT]], axis=1).T
    qtp, kp, kbp, vp, vtp, up = _mixer_in_t(x_prompt.reshape(b * s, d), row(g_norm_mix[li]), w_nat, w_t,
                                            _rope_tables(jnp.arange(s, dtype=I32)), b=b, tm=tm_p, **kw)
    ns = bd * ts
    tm_s = _pick(ns, 512)
    cos_s, sin_s, _, _ = _rope_tables(past + jnp.arange(ts, dtype=I32))
    cos_s = jnp.tile(cos_s, (tm_s // ts, 1))
    sin_s = jnp.tile(sin_s, (tm_s // ts, 1))
    qs, ks, vs, us = _mixer_in(x_sample.reshape(ns, d), row(g_norm_mix[li]), w_in_b, cos_s, sin_s,
                               tm=tm_s, **kw)
    r3 = lambda a, bb: a.reshape(bb, -1, a.shape[-1])
    kp, kbp, vp, up = (r3(a, b) for a in (kp, kbp, vp, up))
    qs, ks, vs, us = (r3(a, bd) for a in (qs, ks, vs, us))

    att_p = _attn_prompt(lams, qtp, kbp, vtp, tq=_pick(s, 256), heads=4, lam_init=lam_init, **kw)
    flat = lambda c: c.reshape(c.shape[0], n_pool, page * ATT_HEADS, LANES)
    att_s = _attn_sample(page_table, lams, qs, ks.reshape(bd, ts * ATT_HEADS, LANES),
                         vs.reshape(bd, ts * ATT_HEADS, LANES), flat(cache_k), flat(cache_v), li,
                         pp=_pick(page_table.shape[1], 16), lam_init=lam_init, **kw)

    xp = _mixer_out(x_prompt, att_p, up, up, mix_w, tb=1, t=tm_p, hist_from_u=True,
                    att_gain=1.0 - lam_init, **kw)
    state = state_conv[li]
    hist_s = jnp.pad(state, ((0, 0), (CONV_HIST - state.shape[1], 0), (0, 0)))
    xs = _mixer_out(x_sample, att_s, us, hist_s, mix_w, tb=_pick(bd, 16), t=ts, hist_from_u=False,
                    att_gain=1.0 - lam_init, **kw)
    conv_p = up[:, s - (CONV_W - 1):]
    conv_s = jnp.concatenate([state, us], axis=1)[:, -(CONV_W - 1):]

    mk, mv = _mem_kv(mem_prompt.reshape(b * mt, d), row(g_mem_src[li]), w_mk[li].astype(BF16),
                     w_mv[li].astype(BF16), tm=_pick(b * mt, 512), **kw)
    mk = mk.reshape(b, mt, d)
    mv = mv.reshape(b, mt, d)
    wq_b, wo_b = w_mq[li].astype(BF16), w_mo[li].astype(BF16)
    xp = _mem_attn(xp, mk, mv, row(g_norm_mem[li]), wq_b, wo_b, tb=1, t=tm_p, **kw)
    xs = _mem_attn(xs, cache_mem_k[li], cache_mem_v[li],
                   row(g_norm_mem[li]), wq_b, wo_b, tb=_pick(bd, 8), t=ts, **kw)

    x_all = jnp.concatenate([xp.reshape(b * s, d), xs.reshape(ns, d)], axis=0)
    n_all = x_all.shape[0]
    sk = sub_keys[li].reshape(PEER_HEADS * 2, N_KEYS, -1).astype(BF16)
    xn, n1, e1, r2, e2 = _peer_route(x_all, row(g_norm_peer[li]), w_pq[li].T.astype(BF16), sk,
                                     tm=_pick(n_all, 256), **kw)
    y = _peer_dense(x_all, xn, n1, e1, r2, e2, u_tab[li].astype(BF16), v_tab[li].T.astype(BF16),
                    row(g_final), tm=_pick(n_all, 512), ti=_pick(N_KEYS, 16), **kw)
    y_prompt = y[:b * s].reshape(b, s, d)
    y_sample = y[b * s:].reshape(bd, ts, d)

    k4 = lambda a: a.reshape(1, a.shape[0], a.shape[1], ATT_HEADS, -1)
    return (y_prompt, y_sample, k4(kp), k4(vp), conv_p[None],
            mk.reshape(1, b, mt, MEM_HEADS, -1), mv.reshape(1, b, mt, MEM_HEADS, -1),
            k4(ks), k4(vs), conv_s[None])


def kernel(x_prompt, x_sample, cache_k, cache_v, state_conv, cache_mem_k, cache_mem_v, page_table, mem_prompt, g_norm_mix, w_in, lam_q1, lam_k1, lam_q2, lam_k2, g_subln, conv_w, conv_b, conv_ln_g, conv_ln_b, w_out, g_norm_mem, g_mem_src, w_mq, w_mk, w_mv, w_mo, g_norm_peer, w_pq, sub_keys, u_tab, v_tab, g_final):
    return _forward(x_prompt, x_sample, cache_k, cache_v, state_conv, cache_mem_k, cache_mem_v, page_table,
                    mem_prompt, g_norm_mix, w_in, lam_q1, lam_k1, lam_q2, lam_k2, g_subln, conv_w, conv_b,
                    conv_ln_g, conv_ln_b, w_out, g_norm_mem, g_mem_src, w_mq, w_mk, w_mv, w_mo, g_norm_peer,
                    w_pq, sub_keys, u_tab, v_tab, g_final)
```

```python
import functools
import math

import jax
import jax.numpy as jnp
from jax import lax
from jax.experimental import pallas as pl
from jax.experimental.pallas import tpu as pltpu

F32 = jnp.float32
BF16 = jnp.bfloat16
I32 = jnp.int32

EPS = 1e-6
ROPE_THETA = 10000.0
ATT_HEADS = 4
ATT_DH = 64
ATT_DV = 128
D_ATT = 512
D_CONV = 512
CONV_W = 31
CONV_HIST = 32
MEM_HEADS = 4
PEER_HEADS = 8
N_KEYS = 128
PEER_TOPK = 16
LANES = 128
VMEM_LIMIT = 56 * 1024 * 1024

_NT = (((1,), (1,)), ((), ()))


def _params(sem, vmem=VMEM_LIMIT):
    return pltpu.CompilerParams(dimension_semantics=sem, vmem_limit_bytes=vmem)


def _rmsnorm(x, g):
    return x * lax.rsqrt(jnp.mean(x * x, axis=-1, keepdims=True) + EPS) * g


def _gelu(x):
    return 0.5 * x * (1.0 + lax.erf(x * (2.0 ** -0.5)))


def _lam(lq1, lk1, lq2, lk2, lam_init):
    a = jnp.sum(lq1 * lk1, axis=-1, keepdims=True)
    b = jnp.sum(lq2 * lk2, axis=-1, keepdims=True)
    return jnp.exp(a) - jnp.exp(b) + lam_init


def _mixer_in_body(x_ref, g_ref, w_ref, cos_ref, sin_ref, q_ref, k_ref, v_ref, u_ref):
    xn = _rmsnorm(x_ref[...], g_ref[...])
    z = jnp.dot(xn.astype(BF16), w_ref[...], preferred_element_type=F32)
    cos = cos_ref[...]
    sin = sin_ref[...]
    lane = lax.broadcasted_iota(I32, (1, LANES), 1)
    first_half = (lane & 32) == 0

    def rope(t):
        partner = jnp.where(first_half, pltpu.roll(t, 96, 1), pltpu.roll(t, 32, 1))
        return t * cos + partner * sin

    for h in range(ATT_HEADS):
        lo = h * LANES
        q_ref[:, lo:lo + LANES] = rope(z[:, lo:lo + LANES])
        k_ref[:, lo:lo + LANES] = rope(z[:, D_ATT + lo:D_ATT + lo + LANES])
    v_ref[...] = z[:, 2 * D_ATT:3 * D_ATT]
    a = z[:, 3 * D_ATT:3 * D_ATT + D_CONV]
    gate = z[:, 3 * D_ATT + D_CONV:]
    u_ref[...] = a * jax.nn.sigmoid(gate)


def _mixer_in_t_body(x_ref, g_ref, w_ref, wt_ref, cos_ref, sin_ref, cost_ref, sint_ref,
                     qt_ref, k_ref, kb_ref, v_ref, vt_ref, u_ref):
    xn = _rmsnorm(x_ref[...], g_ref[...]).astype(BF16)
    z = jnp.dot(xn, w_ref[...], preferred_element_type=F32)
    zt = lax.dot_general(wt_ref[...], xn, _NT, preferred_element_type=F32)
    cos = cos_ref[...]
    sin = sin_ref[...]
    lane = lax.broadcasted_iota(I32, (1, LANES), 1)
    first_half = (lane & 32) == 0
    for h in range(ATT_HEADS):
        t = z[:, h * LANES:(h + 1) * LANES]
        partner = jnp.where(first_half, pltpu.roll(t, 96, 1), pltpu.roll(t, 32, 1))
        kh = t * cos + partner * sin
        k_ref[:, h * LANES:(h + 1) * LANES] = kh
        kb_ref[:, h * LANES:(h + 1) * LANES] = kh.astype(BF16)
    v_ref[...] = z[:, D_ATT:2 * D_ATT]
    u_ref[...] = z[:, 2 * D_ATT:2 * D_ATT + D_CONV] * jax.nn.sigmoid(z[:, 2 * D_ATT + D_CONV:])
    ct = cost_ref[...]
    st = sint_ref[...]
    half = ATT_DH // 2
    qscale = (ATT_DH ** -0.5) * math.log2(math.e)
    for blk in range(2 * ATT_HEADS):
        r0 = blk * ATT_DH
        x1 = zt[r0:r0 + half]
        x2 = zt[r0 + half:r0 + ATT_DH]
        qt_ref[r0:r0 + half, :] = ((x1 * ct - x2 * st) * qscale).astype(BF16)
        qt_ref[r0 + half:r0 + ATT_DH, :] = ((x2 * ct + x1 * st) * qscale).astype(BF16)
    vt_ref[...] = zt[D_ATT:].astype(BF16)


def _mixer_in_t(x2d, g, w_bf16, wt_bf16, tabs, *, b, tm, interpret=False):
    n, d = x2d.shape
    s = n // b
    nt = s // tm
    cos, sin, cost, sint = tabs
    tab = pl.BlockSpec((tm, LANES), lambda i: (i % nt, 0))
    tabt = pl.BlockSpec((ATT_DH // 2, tm), lambda i: (0, i % nt))
    nat = pl.BlockSpec((tm, D_ATT), lambda i: (i, 0))
    fmaj = pl.BlockSpec((None, D_ATT, tm), lambda i: (i // nt, 0, i % nt))
    nat_f32 = jax.ShapeDtypeStruct((n, D_ATT), F32)
    fmaj_b16 = jax.ShapeDtypeStruct((b, D_ATT, s), BF16)
    return pl.pallas_call(
        _mixer_in_t_body,
        out_shape=[fmaj_b16, nat_f32, jax.ShapeDtypeStruct((n, D_ATT), BF16), nat_f32, fmaj_b16, nat_f32],
        grid=(n // tm,),
        in_specs=[pl.BlockSpec((tm, d), lambda i: (i, 0)),
                  pl.BlockSpec((1, d), lambda i: (0, 0)),
                  pl.BlockSpec(w_bf16.shape, lambda i: (0, 0)),
                  pl.BlockSpec(wt_bf16.shape, lambda i: (0, 0)),
                  tab, tab, tabt, tabt],
        out_specs=[fmaj, nat, nat, nat, fmaj, nat],
        compiler_params=_params(("parallel",)),
        name="mixer_in_t", interpret=interpret,
    )(x2d, g, w_bf16, wt_bf16, cos, sin, cost, sint)


def _rope_tables(pos):
    inv = ROPE_THETA ** (-jnp.arange(0, ATT_DH, 2, dtype=F32) / ATT_DH)
    ang = pos.astype(F32)[:, None] * inv[None, :]
    cos = jnp.tile(jnp.cos(ang), (1, 4))
    sin = jnp.sin(ang)
    sin4 = jnp.concatenate([-sin, sin, -sin, sin], axis=1)
    return cos, sin4, jnp.cos(ang).T, sin.T


def _mixer_in(x2d, g, w_bf16, cos, sin, *, tm, interpret=False):
    n, d = x2d.shape
    d_in = w_bf16.shape[1]
    nt = cos.shape[0] // tm
    tab = pl.BlockSpec((tm, LANES), lambda i: (i % nt, 0))
    out = pl.BlockSpec((tm, D_ATT), lambda i: (i, 0))
    return pl.pallas_call(
        _mixer_in_body,
        out_shape=[jax.ShapeDtypeStruct((n, D_ATT), F32)] * 4,
        grid=(n // tm,),
        in_specs=[pl.BlockSpec((tm, d), lambda i: (i, 0)),
                  pl.BlockSpec((1, d), lambda i: (0, 0)),
                  pl.BlockSpec((d, d_in), lambda i: (0, 0)),
                  tab, tab],
        out_specs=[out] * 4,
        compiler_params=_params(("parallel",)),
        name="mixer_in", interpret=interpret,
    )(x2d, g, w_bf16, cos, sin)


def _attn_prompt_body(lq1, lk1, lq2, lk2, qt_ref, k_ref, vt_ref, o_ref, *, tq, lam_init):
    qi = pl.program_id(2)
    lam = _lam(lq1[...], lk1[...], lq2[...], lk2[...], lam_init)
    heads = qt_ref.shape[0] // LANES
    feat = lax.broadcasted_iota(I32, (LANES, tq), 0)
    zero = jnp.zeros((), BF16)
    qq = []
    for g in range(heads):
        qt = qt_ref[g * LANES:(g + 1) * LANES, :]
        qq.append(jnp.concatenate([jnp.where(feat < ATT_DH, qt, zero),
                                   jnp.where(feat >= ATT_DH, qt, zero)], axis=1))

    def step(j, carry, masked):
        start = pl.multiple_of(j * tq, tq)
        out = []
        scores = [jnp.dot(k_ref[pl.ds(start, tq), g * LANES:(g + 1) * LANES], qq[g],
                          preferred_element_type=F32) for g in range(heads)]
        for g in range(heads):
            m, l, acc = carry[g]
            vj = vt_ref[g * LANES:(g + 1) * LANES, pl.ds(start, tq)]
            s = scores[g]
            if masked:
                kpos = lax.broadcasted_iota(I32, s.shape, 0)
                qpos = lax.broadcasted_iota(I32, s.shape, 1) & (tq - 1)
                s = jnp.where(kpos <= qpos, s, -jnp.inf)
            m_new = jnp.maximum(m, jnp.max(s, axis=0, keepdims=True))
            alpha = jnp.exp2(m - m_new)
            p = jnp.exp2(s - m_new)
            l = l * alpha + jnp.sum(p, axis=0, keepdims=True)
            acc = acc * alpha + jnp.dot(vj, p.astype(BF16), preferred_element_type=F32)
            out.append((m_new, l, acc))
        return tuple(out)

    init = tuple((jnp.full((1, 2 * tq), -jnp.inf, F32), jnp.zeros((1, 2 * tq), F32),
                  jnp.zeros((ATT_DV, 2 * tq), F32)) for _ in range(heads))
    carry = lax.fori_loop(0, qi, lambda j, c: step(j, c, False), init)
    carry = step(qi, carry, True)
    for g in range(heads):
        _, l, acc = carry[g]
        o = acc / l
        o_ref[:, g * LANES:(g + 1) * LANES] = (o[:, :tq] - lam * o[:, tq:]).T


def _attn_prompt(lams, qt, kb, vt, *, tq, heads, lam_init, interpret=False):
    b, s, _ = kb.shape
    assert tq & (tq - 1) == 0 and ATT_HEADS % heads == 0
    lam_spec = pl.BlockSpec((1, ATT_DH), lambda bi, h, i: (0, 0))
    w = heads * LANES
    return pl.pallas_call(
        functools.partial(_attn_prompt_body, tq=tq, lam_init=lam_init),
        out_shape=jax.ShapeDtypeStruct((b, s, D_ATT), F32),
        grid=(b, ATT_HEADS // heads, s // tq),
        in_specs=[lam_spec] * 4 + [pl.BlockSpec((None, w, tq), lambda bi, h, i: (bi, h, i)),
                                   pl.BlockSpec((None, s, w), lambda bi, h, i: (bi, 0, h)),
                                   pl.BlockSpec((None, w, s), lambda bi, h, i: (bi, h, 0))],
        out_specs=pl.BlockSpec((None, tq, w), lambda bi, h, i: (bi, i, h)),
        compiler_params=_params(("parallel", "parallel", "arbitrary")),
        name="attn_prompt", interpret=interpret,
    )(*lams, qt, kb, vt)


def _attn_sample_body(pt_ref, lq1, lk1, lq2, lk2, q_ref, kn_ref, vn_ref, *rest, pp, t, lam_init):
    k_refs = rest[:pp]
    v_refs = rest[pp:2 * pp]
    o_ref = rest[2 * pp]
    qm_ref, m_ref, l_ref, acc_ref = rest[2 * pp + 1:]
    g = pl.program_id(1)
    rows = ATT_HEADS * 2 * t
    head_shift = (2 * t).bit_length() - 1

    @pl.when(g == 0)
    def _():
        q = q_ref[...] * (ATT_DH ** -0.5)
        lane = lax.broadcasted_iota(I32, (t, LANES), 1)
        blocks = []
        for h in range(ATT_HEADS):
            qh = q[:, h * LANES:(h + 1) * LANES]
            blocks.append(jnp.where(lane < ATT_DH, qh, 0.0))
            blocks.append(jnp.where(lane >= ATT_DH, qh, 0.0))
        qm_ref[...] = jnp.concatenate(blocks, axis=0).astype(BF16)
        m_ref[...] = jnp.full(m_ref.shape, -jnp.inf, F32)
        l_ref[...] = jnp.zeros(l_ref.shape, F32)
        acc_ref[...] = jnp.zeros(acc_ref.shape, F32)

    def online(s_list, v_list):
        m_prev = m_ref[...]
        m_new = m_prev
        for s in s_list:
            m_new = jnp.maximum(m_new, jnp.max(s, axis=-1, keepdims=True))
        alpha = jnp.exp(m_prev - m_new)
        l = l_ref[...] * alpha
        acc = acc_ref[...] * alpha
        for s, v in zip(s_list, v_list):
            p = jnp.exp(s - m_new)
            l = l + jnp.sum(p, axis=-1, keepdims=True)
            acc = acc + jnp.dot(p.astype(BF16), v, preferred_element_type=F32)
        l_ref[...] = l
        acc_ref[...] = acc
        m_ref[...] = m_new

    qm = qm_ref[...]
    nrow = k_refs[0].shape[0]
    same_head = ((lax.broadcasted_iota(I32, (rows, nrow), 0) >> head_shift)
                 == (lax.broadcasted_iota(I32, (rows, nrow), 1) & (ATT_HEADS - 1)))
    online([jnp.where(same_head,
                      lax.dot_general(qm, k_refs[j][...].astype(BF16), _NT, preferred_element_type=F32),
                      -jnp.inf) for j in range(pp)],
           [v_refs[j][...].astype(BF16) for j in range(pp)])

    @pl.when(g == pl.num_programs(1) - 1)
    def _():
        s2 = lax.dot_general(qm, kn_ref[...].astype(BF16), _NT, preferred_element_type=F32)
        r = lax.broadcasted_iota(I32, s2.shape, 0)
        c = lax.broadcasted_iota(I32, s2.shape, 1)
        ok = ((r >> head_shift) == (c & (ATT_HEADS - 1))) & ((c >> 2) <= (r & (t - 1)))
        online([jnp.where(ok, s2, -jnp.inf)], [vn_ref[...].astype(BF16)])
        lam = _lam(lq1[...], lk1[...], lq2[...], lk2[...], lam_init)
        o = acc_ref[...] / l_ref[...]
        for h in range(ATT_HEADS):
            oh = o[h * 2 * t:(h + 1) * 2 * t]
            o_ref[:, h * LANES:(h + 1) * LANES] = oh[:t] - lam * oh[t:]


def _attn_sample(page_table, lams, q, k_new, v_new, cache_k, cache_v, li, *, pp, lam_init, interpret=False):
    bd, t, _ = q.shape
    n_pages = page_table.shape[1]
    prow = cache_k.shape[2]
    assert n_pages % pp == 0 and t & (t - 1) == 0 and ATT_HEADS == 4
    rows = ATT_HEADS * 2 * t
    lam_spec = pl.BlockSpec((1, ATT_DH), lambda b, g, pt: (0, 0))
    tok_spec = pl.BlockSpec((None, t, D_ATT), lambda b, g, pt: (b, 0, 0))
    new_spec = pl.BlockSpec((None, t * ATT_HEADS, LANES), lambda b, g, pt: (b, 0, 0))

    def page_spec(j):
        return pl.BlockSpec((None, None, prow, LANES), lambda b, g, pt: (li, pt[b, g * pp + j], 0, 0))

    grid_spec = pltpu.PrefetchScalarGridSpec(
        num_scalar_prefetch=1,
        grid=(bd, n_pages // pp),
        in_specs=[lam_spec] * 4 + [tok_spec, new_spec, new_spec] + [page_spec(j) for j in range(pp)] * 2,
        out_specs=tok_spec,
        scratch_shapes=[pltpu.VMEM((rows, LANES), BF16), pltpu.VMEM((rows, 1), F32),
                        pltpu.VMEM((rows, 1), F32), pltpu.VMEM((rows, LANES), F32)],
    )
    return pl.pallas_call(
        functools.partial(_attn_sample_body, pp=pp, t=t, lam_init=lam_init),
        out_shape=jax.ShapeDtypeStruct((bd, t, D_ATT), F32),
        grid_spec=grid_spec,
        compiler_params=_params(("parallel", "arbitrary")),
        name="attn_sample", interpret=interpret,
    )(page_table, *lams, q, k_new, v_new, *([cache_k] * pp), *([cache_v] * pp))


def _mixer_out_body(x_ref, att_ref, u_ref, hist_ref, gsub_ref, cw_ref, cb_ref, lng_ref, lnb_ref,
                    wout_ref, o_ref, ucat_ref, ush_ref, conv_ref, *, tb, t, zero_first_hist, att_gain):
    hist = hist_ref[...]
    if zero_first_hist:
        hist = jnp.where(pl.program_id(1) == 0, 0.0, hist)
    ucat_ref[:, 0:CONV_HIST, :] = hist
    ucat_ref[:, CONV_HIST:CONV_HIST + t, :] = u_ref[...]
    rc = min(t, 64)
    off = CONV_HIST - (CONV_W - 1)
    span = t + CONV_HIST - 8
    for r in range(1, 8):
        ush_ref[r - 1, :, 0:span, :] = ucat_ref[:, r:r + span, :]
    for b in range(tb):
        for r0 in range(0, t, rc):
            acc = jnp.broadcast_to(cb_ref[...], (rc, D_CONV))
            for j in range(CONV_W):
                r = (j + off) % 8
                base = r0 + j + off - r
                src = ucat_ref[b, base:base + rc, :] if r == 0 else ush_ref[r - 1, b, base:base + rc, :]
                acc = acc + cw_ref[j:j + 1, :] * src
            conv_ref[b * t + r0:b * t + r0 + rc, :] = acc
    c = conv_ref[...]
    mu = jnp.mean(c, axis=-1, keepdims=True)
    cc = c - mu
    var = jnp.mean(cc * cc, axis=-1, keepdims=True)
    cn = cc * lax.rsqrt(var + EPS) * lng_ref[...] + lnb_ref[...]
    cact = cn * jax.nn.sigmoid(cn)
    m = tb * t
    att = att_ref[...].reshape(m, D_ATT)
    gsub = gsub_ref[...] * att_gain
    a = jnp.concatenate(
        [_rmsnorm(att[:, h * LANES:(h + 1) * LANES], gsub) for h in range(ATT_HEADS)], axis=1)
    y = jnp.dot(a.astype(BF16), wout_ref[0:D_ATT, :], preferred_element_type=F32)
    y = y + jnp.dot(cact.astype(BF16), wout_ref[D_ATT:, :], preferred_element_type=F32)
    o_ref[...] = (y + x_ref[...].reshape(m, -1)).reshape(o_ref.shape)


def _mixer_out(x, att, u, hist, weights, *, tb, t, hist_from_u, att_gain, interpret=False):
    b, tt, d = x.shape
    gsub, cw, cb, lng, lnb, wout = weights
    nb = t // CONV_HIST
    if hist_from_u:
        hist_spec = pl.BlockSpec((tb, CONV_HIST, D_CONV),
                                 lambda bi, i: (bi, jnp.maximum(i * nb - 1, 0), 0))
    else:
        hist_spec = pl.BlockSpec((tb, CONV_HIST, D_CONV), lambda bi, i: (bi, 0, 0))
    row = lambda w: pl.BlockSpec((1, w), lambda bi, i: (0, 0))
    tile = lambda w: pl.BlockSpec((tb, t, w), lambda bi, i: (bi, i, 0))
    return pl.pallas_call(
        functools.partial(_mixer_out_body, tb=tb, t=t, zero_first_hist=hist_from_u, att_gain=att_gain),
        out_shape=jax.ShapeDtypeStruct(x.shape, F32),
        grid=(b // tb, tt // t),
        in_specs=[tile(d), tile(D_ATT), tile(D_CONV), hist_spec, row(LANES),
                  pl.BlockSpec((CONV_W, D_CONV), lambda bi, i: (0, 0)),
                  row(D_CONV), row(D_CONV), row(D_CONV),
                  pl.BlockSpec((d, d), lambda bi, i: (0, 0))],
        out_specs=tile(d),
        scratch_shapes=[pltpu.VMEM((tb, CONV_HIST + t, D_CONV), F32),
                        pltpu.VMEM((7, tb, CONV_HIST + t - 8, D_CONV), F32),
                        pltpu.VMEM((tb * t, D_CONV), F32)],
        compiler_params=_params(("parallel", "arbitrary")),
        name="mixer_out", interpret=interpret,
    )(x, att, u, hist, gsub, cw, cb, lng, lnb, wout)


def _mem_kv_body(m_ref, g_ref, wk_ref, wv_ref, k_ref, v_ref):
    mn = _rmsnorm(m_ref[...], g_ref[...]).astype(BF16)
    k_ref[...] = jnp.dot(mn, wk_ref[...], preferred_element_type=F32)
    v_ref[...] = jnp.dot(mn, wv_ref[...], preferred_element_type=F32)


def _mem_kv(mem2d, g, wk, wv, *, tm, interpret=False):
    n, d = mem2d.shape
    tile = pl.BlockSpec((tm, d), lambda i: (i, 0))
    full = pl.BlockSpec((d, d), lambda i: (0, 0))
    return pl.pallas_call(
        _mem_kv_body,
        out_shape=[jax.ShapeDtypeStruct((n, d), F32)] * 2,
        grid=(n // tm,),
        in_specs=[tile, pl.BlockSpec((1, d), lambda i: (0, 0)), full, full],
        out_specs=[tile, tile],
        compiler_params=_params(("parallel",)),
        name="mem_kv", interpret=interpret,
    )(mem2d, g, wk, wv)


def _mem_attn_body(x_ref, g_ref, wq_ref, wo_ref, *rest, tb, t, per_head):
    mk_ref, mv_ref, o_ref = rest[0], rest[1], rest[-1]
    m = tb * t
    d = x_ref.shape[-1]
    dh = d // MEM_HEADS
    x = x_ref[...].reshape(m, d)
    hn = _rmsnorm(x, g_ref[...]).astype(BF16)
    q = (jnp.dot(hn, wq_ref[...], preferred_element_type=F32) * (dh ** -0.5)).astype(BF16)
    outs = []
    for b in range(tb):
        heads = []
        for h in range(MEM_HEADS):
            qh = q[b * t:(b + 1) * t, h * dh:(h + 1) * dh]
            if per_head:
                kh = mk_ref[b, :, h, :].astype(BF16)
                vh = mv_ref[b, :, h, :].astype(BF16)
            else:
                kh = mk_ref[b, :, h * dh:(h + 1) * dh].astype(BF16)
                vh = mv_ref[b, :, h * dh:(h + 1) * dh].astype(BF16)
            s = lax.dot_general(qh, kh, _NT, preferred_element_type=F32)
            e = jnp.exp(s - jnp.max(s, axis=-1, keepdims=True))
            p = e / jnp.sum(e, axis=-1, keepdims=True)
            heads.append(jnp.dot(p.astype(BF16), vh, preferred_element_type=F32))
        outs.append(jnp.concatenate(heads, axis=1))
    o = jnp.concatenate(outs, axis=0) if tb > 1 else outs[0]
    y = jnp.dot(o.astype(BF16), wo_ref[...], preferred_element_type=F32) + x
    o_ref[...] = y


def _mem_attn(x, mk, mv, g, wq, wo, *, tb, t, n_rows, row0=0, into=None, interpret=False):
    b, tt, d = x.shape
    mt = mk.shape[1]
    m = tb * t
    nt = tt // t
    assert row0 % m == 0
    tile = pl.BlockSpec((tb, t, d), lambda bi, i: (bi, i, 0))
    per_head = mk.ndim == 4
    if per_head:
        mem = [pl.BlockSpec((tb, mt) + mk.shape[2:], lambda bi, i: (bi, 0, 0, 0))]
    else:
        mem = [pl.BlockSpec((tb, mt, d), lambda bi, i: (bi, 0, 0))]
    full = pl.BlockSpec((d, d), lambda bi, i: (0, 0))
    in_specs = [tile, pl.BlockSpec((1, d), lambda bi, i: (0, 0)), full, full] + mem + mem
    args = [x, g, wq, wo] + [mk] * len(mem) + [mv] * len(mem)
    aliases = {}
    if into is not None:
        in_specs.append(pl.BlockSpec(memory_space=pl.ANY))
        args.append(into)
        aliases = {len(args) - 1: 0}
    return pl.pallas_call(
        functools.partial(_mem_attn_body, tb=tb, t=t, per_head=per_head),
        out_shape=jax.ShapeDtypeStruct((n_rows, d), F32),
        grid=(b // tb, nt),
        in_specs=in_specs,
        out_specs=pl.BlockSpec((m, d), lambda bi, i: (row0 // m + bi * nt + i, 0)),
        input_output_aliases=aliases,
        compiler_params=_params(("parallel", "arbitrary")),
        name="mem_attn", interpret=interpret,
    )(*args)


def _topk16(s, want_rank):
    r = s.shape[0]
    iota = lax.broadcasted_iota(I32, s.shape, 0)
    vals, idxs = [], []
    rank = jnp.full(s.shape, PEER_TOPK, I32) if want_rank else None
    for k in range(PEER_TOPK):
        m = jnp.max(s, axis=0, keepdims=True)
        idx = jnp.min(jnp.where(s == m, iota, r), axis=0, keepdims=True)
        hit = iota == idx
        if want_rank:
            rank = jnp.where(hit, k, rank)
        s = jnp.where(hit, -jnp.inf, s)
        vals.append(m)
        idxs.append(idx)
    return jnp.concatenate(vals, axis=0), jnp.concatenate(idxs, axis=0), rank


def _route_exact(s1, s2):
    sv1, _, r1 = _topk16(s1, True)
    sv2, _, r2 = _topk16(s2, True)
    cand = jnp.concatenate([sv1[k:k + 1] + sv2 for k in range(PEER_TOPK)], axis=0)
    tv, tp, _ = _topk16(cand, False)
    k1 = tp >> 4
    n1 = jnp.zeros(s1.shape, F32)
    for k in range(PEER_TOPK):
        n1 = n1 + jnp.where(r1 == k1[k:k + 1], 1.0, 0.0)
    z = jnp.sum(jnp.exp(tv - tv[0:1]), axis=0, keepdims=True)
    return n1, r2, z


def _topk16_distinct(s, want_rank):
    vals = []
    rank = jnp.full(s.shape, PEER_TOPK, I32) if want_rank else None
    for k in range(PEER_TOPK):
        m = jnp.max(s, axis=0, keepdims=True)
        hit = s == m
        if want_rank:
            rank = jnp.where(hit, k, rank)
        s = jnp.where(hit, -jnp.inf, s)
        vals.append(m)
    return vals, rank, s


_CAND_GROUPS = (
    ((0, 0, 0, 8),),
    ((0, 0, 8, 8),),
    ((1, 0, 0, 8),),
    ((2, 0, 0, 5), (4, 5, 0, 3)),
    ((3, 0, 0, 4), (5, 4, 0, 2), (6, 6, 0, 2)),
    ((7, 0, 0, 2),) + tuple((8 + i, 2 + i, 0, 1) for i in range(6)),
    ((14, 0, 0, 1), (15, 1, 0, 1)),
)


def _route_distinct(s1, s2):
    t = s1.shape[1]
    sv1, r1, _ = _topk16_distinct(s1, True)
    sv2, r2, _ = _topk16_distinct(s2, True)
    sv2_lo = jnp.concatenate(sv2[:8], axis=0)
    sv2_hi = jnp.concatenate(sv2[8:], axis=0)
    row = lax.broadcasted_iota(I32, (8, t), 0)
    groups = []
    for pieces in _CAND_GROUPS:
        val = jnp.full((8, t), -jnp.inf, F32)
        for k1, off, k2, nk in pieces:
            src = sv2_hi if k2 else sv2_lo
            if off:
                src = pltpu.roll(src, off, 0)
            val = jnp.where((row >= off) & (row < off + nk), sv1[k1] + src, val)
        groups.append(val)
    cand = jnp.concatenate(groups, axis=0)
    tv, _, left = _topk16_distinct(cand, False)
    won = jnp.where(left != cand, 1.0, 0.0)
    cnt = [jnp.zeros((1, t), F32) for _ in range(PEER_TOPK)]
    for gi, pieces in enumerate(_CAND_GROUPS):
        w = won[gi * 8:(gi + 1) * 8]
        for k1, off, _, nk in pieces:
            cnt[k1] = cnt[k1] + jnp.sum(jnp.where((row >= off) & (row < off + nk), w, 0.0),
                                        axis=0, keepdims=True)
    n1 = jnp.zeros(s1.shape, F32)
    for k in range(PEER_TOPK):
        n1 = jnp.where(r1 == k, cnt[k], n1)
    z = jnp.zeros((1, t), F32)
    for k in range(PEER_TOPK):
        z = z + jnp.exp(tv[k] - tv[0])
    picked = lambda r: jnp.sum(jnp.where(r < PEER_TOPK, 1.0, 0.0), axis=0, keepdims=True)
    total = cnt[0]
    for k in range(1, PEER_TOPK):
        total = total + cnt[k]
    tied = (picked(r1) != PEER_TOPK) | (picked(r2) != PEER_TOPK) | (total != PEER_TOPK)
    return n1, r2, z, sv1[0], sv2[0], tied


def _peer_route_body(x_ref, g_ref, wpq_ref, sk_ref, xn_ref, n1_ref, e1_ref, r2_ref, e2_ref, q_ref):
    xnt = _rmsnorm(x_ref[...], g_ref[...]).T.astype(BF16)
    xn_ref[...] = xnt
    q_ref[...] = jnp.dot(wpq_ref[...], xnt, preferred_element_type=F32).astype(BF16)

    group = 2

    def heads(hg, carry):
        def scores(hc):
            r0 = pl.multiple_of(hc * LANES, LANES)
            return jnp.dot(sk_ref[hc], q_ref[pl.ds(r0, LANES), :], preferred_element_type=F32)

        done = []
        any_tied = None
        for hh in range(group):
            h = hg * group + hh
            s1 = scores(h * 2)
            s2 = scores(h * 2 + 1)
            n1, r2, z, top1, top2, tied = _route_distinct(s1, s2)
            ex2 = jnp.exp(s2 - top2)
            n1_ref[h] = n1
            e1_ref[h] = jnp.exp(s1 - top1)
            r2_ref[h] = r2.astype(BF16)
            e2_ref[h] = (ex2 / z).astype(BF16)
            done.append((h, s1, s2, ex2))
            any_tied = tied if any_tied is None else any_tied | tied

        @pl.when(jnp.max(jnp.where(any_tied, 1.0, 0.0)) > 0.5)
        def _():
            for h, s1, s2, ex2 in done:
                n1x, r2x, zx = _route_exact(s1, s2)
                n1_ref[h] = n1x
                r2_ref[h] = r2x.astype(BF16)
                e2_ref[h] = (ex2 / zx).astype(BF16)

        return carry

    lax.fori_loop(0, PEER_HEADS // group, heads, 0)


def _peer_route(x2d, g, wpq, sk, *, tm, interpret=False):
    n, d = x2d.shape
    dq = wpq.shape[0]
    route = pl.BlockSpec((PEER_HEADS, N_KEYS, tm), lambda i: (0, 0, i))
    route_f32 = jax.ShapeDtypeStruct((PEER_HEADS, N_KEYS, n), F32)
    route_b16 = jax.ShapeDtypeStruct((PEER_HEADS, N_KEYS, n), BF16)
    return pl.pallas_call(
        _peer_route_body,
        out_shape=[jax.ShapeDtypeStruct((d, n), BF16), route_f32, route_f32, route_b16, route_b16],
        grid=(n // tm,),
        in_specs=[pl.BlockSpec((tm, d), lambda i: (i, 0)),
                  pl.BlockSpec((1, d), lambda i: (0, 0)),
                  pl.BlockSpec((dq, d), lambda i: (0, 0)),
                  pl.BlockSpec(sk.shape, lambda i: (0, 0, 0))],
        out_specs=[pl.BlockSpec((d, tm), lambda i: (0, i))] + [route] * 4,
        scratch_shapes=[pltpu.VMEM((dq, tm), BF16)],
        compiler_params=_params(("parallel",)),
        name="peer_route", interpret=interpret,
    )(x2d, g, wpq, sk)


def _row_bf16(row):
    slab = jnp.broadcast_to(row, (16, row.shape[1])).astype(BF16)
    return jnp.concatenate([slab] * (N_KEYS // 16), axis=0)


def _peer_dense_body(x_ref, xn_ref, n1_ref, e1_ref, r2_ref, e2_ref, u_ref, vt_ref, gf_ref, y_ref, y2_ref,
                     acc_ref, w_ref, *, ti, tiles_first):
    j = pl.program_id(1)

    @pl.when(j == 0)
    def _():
        acc_ref[...] = jnp.zeros(acc_ref.shape, F32)

    xn = xn_ref[...]
    ht = [jnp.dot(u_ref[il * N_KEYS:(il + 1) * N_KEYS, :], xn, preferred_element_type=F32)
          for il in range(ti)]
    out = None
    kc = 2 * N_KEYS
    assert ti % 8 == 0
    for il in range(ti):
        slab = pl.ds(pl.multiple_of(j * ti + (il // 8) * 8, 8), 8)
        sub = il % 8
        rows = slice(il * N_KEYS, (il + 1) * N_KEYS)
        gate = None
        for h in range(PEER_HEADS):
            n1 = _row_bf16(n1_ref[h, slab, :][sub:sub + 1])
            e1 = _row_bf16(e1_ref[h, slab, :][sub:sub + 1])
            term = jnp.where(r2_ref[h] < n1, e2_ref[h], jnp.zeros((), BF16)) * e1
            gate = term if gate is None else gate + term
        w_ref[rows, :] = _gelu(ht[il]).astype(BF16) * gate
        if (il + 1) * N_KEYS % kc == 0:
            k0 = (il + 1) * N_KEYS - kc
            part = jnp.dot(vt_ref[:, k0:k0 + kc], w_ref[k0:k0 + kc, :], preferred_element_type=F32)
            out = part if out is None else out + part
    acc_ref[...] += out

    last = j == pl.num_programs(1) - 1
    first_group = pl.program_id(0) < tiles_first

    @pl.when(last & first_group)
    def _():
        y_ref[...] = _rmsnorm(x_ref[...] + acc_ref[...].T, gf_ref[...])

    @pl.when(last & jnp.logical_not(first_group))
    def _():
        y2_ref[...] = _rmsnorm(x_ref[...] + acc_ref[...].T, gf_ref[...])


def _peer_dense(x2d, xn, n1, e1, r2, e2, u_bf16, vt_bf16, g_final, *, n_first, tm, ti, interpret=False):
    n, d = x2d.shape
    ne = u_bf16.shape[0]
    te = ti * N_KEYS
    assert n_first % tm == 0 and (n - n_first) % tm == 0 and n > n_first
    tf = n_first // tm
    tok = pl.BlockSpec((tm, d), lambda i, j: (i, 0))
    route = pl.BlockSpec((PEER_HEADS, N_KEYS, tm), lambda i, j: (0, 0, i))
    return pl.pallas_call(
        functools.partial(_peer_dense_body, ti=ti, tiles_first=tf),
        out_shape=[jax.ShapeDtypeStruct((n_first, d), F32), jax.ShapeDtypeStruct((n - n_first, d), F32)],
        grid=(n // tm, ne // te),
        in_specs=[tok, pl.BlockSpec((d, tm), lambda i, j: (0, i)), route, route, route, route,
                  pl.BlockSpec((te, d), lambda i, j: (j, 0)),
                  pl.BlockSpec((d, te), lambda i, j: (0, j)),
                  pl.BlockSpec((1, d), lambda i, j: (0, 0))],
        out_specs=[pl.BlockSpec((tm, d), lambda i, j: (jnp.minimum(i, tf - 1), 0)),
                   pl.BlockSpec((tm, d), lambda i, j: (jnp.maximum(i - tf, 0), 0))],
        scratch_shapes=[pltpu.VMEM((d, tm), F32), pltpu.VMEM((te, tm), BF16)],
        compiler_params=_params(("parallel", "arbitrary")),
        name="peer_dense", interpret=interpret,
    )(x2d, xn, n1, e1, r2, e2, u_bf16, vt_bf16, g_final)


def _pick(n, pref):
    t = min(n, pref)
    while n % t:
        t //= 2
    return t


def _forward(x_prompt, x_sample, cache_k, cache_v, state_conv, cache_mem_k, cache_mem_v, page_table,
             mem_prompt, g_norm_mix, w_in, lam_q1, lam_k1, lam_q2, lam_k2, g_subln, conv_w, conv_b,
             conv_ln_g, conv_ln_b, w_out, g_norm_mem, g_mem_src, w_mq, w_mk, w_mv, w_mo, g_norm_peer,
             w_pq, sub_keys, u_tab, v_tab, g_final, *, interpret=False):
    depth = w_in.shape[0]
    assert depth == 1, "single layer: the final norm is fused into the PEER kernel"
    li = 0
    b, s, d = x_prompt.shape
    bd, ts, _ = x_sample.shape
    past = page_table.shape[1] * cache_k.shape[2]
    n_pool, page = cache_k.shape[1], cache_k.shape[2]
    mt = mem_prompt.shape[1]
    kw = dict(interpret=interpret)
    row = lambda a: a.reshape(1, -1)

    lam_init = 0.8 - 0.6 * math.exp(-0.3 * li)
    lams = (row(lam_q1[li]), row(lam_k1[li]), row(lam_q2[li]), row(lam_k2[li]))
    w_in_b = w_in[li].astype(BF16)
    mix_w = (row(g_subln[li]), conv_w[li], row(conv_b[li]), row(conv_ln_g[li]), row(conv_ln_b[li]),
             w_out[li].astype(BF16))

    tm_p = _pick(s, 512)
    w_nat = w_in_b[:, D_ATT:]
    w_t = jnp.concatenate([w_in_b[:, :D_ATT], w_in_b[:, 2 * D_ATT:3 * D_ATT]], axis=1).T
    qtp, kp, kbp, vp, vtp, up = _mixer_in_t(x_prompt.reshape(b * s, d), row(g_norm_mix[li]), w_nat, w_t,
                                            _rope_tables(jnp.arange(s, dtype=I32)), b=b, tm=tm_p, **kw)
    ns = bd * ts
    tm_s = _pick(ns, 512)
    cos_s, sin_s, _, _ = _rope_tables(past + jnp.arange(ts, dtype=I32))
    cos_s = jnp.tile(cos_s, (tm_s // ts, 1))
    sin_s = jnp.tile(sin_s, (tm_s // ts, 1))
    qs, ks, vs, us = _mixer_in(x_sample.reshape(ns, d), row(g_norm_mix[li]), w_in_b, cos_s, sin_s,
                               tm=tm_s, **kw)
    r3 = lambda a, bb: a.reshape(bb, -1, a.shape[-1])
    kp, kbp, vp, up = (r3(a, b) for a in (kp, kbp, vp, up))
    qs, ks, vs, us = (r3(a, bd) for a in (qs, ks, vs, us))

    att_p = _attn_prompt(lams, qtp, kbp, vtp, tq=_pick(s, 256), heads=4, lam_init=lam_init, **kw)
    flat = lambda c: c.reshape(c.shape[0], n_pool, page * ATT_HEADS, LANES)
    att_s = _attn_sample(page_table, lams, qs, ks.reshape(bd, ts * ATT_HEADS, LANES),
                         vs.reshape(bd, ts * ATT_HEADS, LANES), flat(cache_k), flat(cache_v), li,
                         pp=_pick(page_table.shape[1], 16), lam_init=lam_init, **kw)

    xp = _mixer_out(x_prompt, att_p, up, up, mix_w, tb=1, t=tm_p, hist_from_u=True,
                    att_gain=1.0 - lam_init, **kw)
    state = state_conv[li]
    hist_s = jnp.pad(state, ((0, 0), (CONV_HIST - state.shape[1], 0), (0, 0)))
    xs = _mixer_out(x_sample, att_s, us, hist_s, mix_w, tb=_pick(bd, 16), t=ts, hist_from_u=False,
                    att_gain=1.0 - lam_init, **kw)
    conv_p = up[:, s - (CONV_W - 1):]
    conv_s = jnp.concatenate([state, us], axis=1)[:, -(CONV_W - 1):]

    mk, mv = _mem_kv(mem_prompt.reshape(b * mt, d), row(g_mem_src[li]), w_mk[li].astype(BF16),
                     w_mv[li].astype(BF16), tm=_pick(b * mt, 512), **kw)
    mk = mk.reshape(b, mt, d)
    mv = mv.reshape(b, mt, d)
    wq_b, wo_b = w_mq[li].astype(BF16), w_mo[li].astype(BF16)
    n_all = b * s + ns
    x_all = _mem_attn(xp, mk, mv, row(g_norm_mem[li]), wq_b, wo_b, tb=1, t=tm_p, n_rows=n_all, **kw)
    x_all = _mem_attn(xs, cache_mem_k[li], cache_mem_v[li], row(g_norm_mem[li]), wq_b, wo_b,
                      tb=_pick(bd, 8), t=ts, n_rows=n_all, row0=b * s, into=x_all, **kw)

    sk = sub_keys[li].reshape(PEER_HEADS * 2, N_KEYS, -1).astype(BF16)
    xn, n1, e1, r2, e2 = _peer_route(x_all, row(g_norm_peer[li]), w_pq[li].T.astype(BF16), sk,
                                     tm=_pick(n_all, 256), **kw)
    y_prompt, y_sample = _peer_dense(x_all, xn, n1, e1, r2, e2, u_tab[li].astype(BF16),
                                     v_tab[li].T.astype(BF16), row(g_final), n_first=b * s,
                                     tm=_pick(math.gcd(b * s, ns), 512), ti=_pick(N_KEYS, 16), **kw)
    y_prompt = y_prompt.reshape(b, s, d)
    y_sample = y_sample.reshape(bd, ts, d)

    k4 = lambda a: a.reshape(1, a.shape[0], a.shape[1], ATT_HEADS, -1)
    return (y_prompt, y_sample, k4(kp), k4(vp), conv_p[None],
            mk.reshape(1, b, mt, MEM_HEADS, -1), mv.reshape(1, b, mt, MEM_HEADS, -1),
            k4(ks), k4(vs), conv_s[None])


def kernel(x_prompt, x_sample, cache_k, cache_v, state_conv, cache_mem_k, cache_mem_v, page_table, mem_prompt, g_norm_mix, w_in, lam_q1, lam_k1, lam_q2, lam_k2, g_subln, conv_w, conv_b, conv_ln_g, conv_ln_b, w_out, g_norm_mem, g_mem_src, w_mq, w_mk, w_mv, w_mo, g_norm_peer, w_pq, sub_keys, u_tab, v_tab, g_final):
    return _forward(x_prompt, x_sample, cache_k, cache_v, state_conv, cache_mem_k, cache_mem_v, page_table,
                    mem_prompt, g_norm_mix, w_in, lam_q1, lam_k1, lam_q2, lam_k2, g_subln, conv_w, conv_b,
                    conv_ln_g, conv_ln_b, w_out, g_norm_mem, g_mem_src, w_mq, w_mk, w_mv, w_mo, g_norm_peer,
                    w_pq, sub_keys, u_tab, v_tab, g_final)
```

```python
import functools
import math

import jax
import jax.numpy as jnp
from jax import lax
from jax.experimental import pallas as pl
from jax.experimental.pallas import tpu as pltpu

F32 = jnp.float32
BF16 = jnp.bfloat16
I32 = jnp.int32

EPS = 1e-6
ROPE_THETA = 10000.0
ATT_HEADS = 4
ATT_DH = 64
ATT_DV = 128
D_ATT = 512
D_CONV = 512
CONV_W = 31
CONV_HIST = 32
MEM_HEADS = 4
PEER_HEADS = 8
N_KEYS = 128
PEER_TOPK = 16
LANES = 128
VMEM_LIMIT = 56 * 1024 * 1024

_NT = (((1,), (1,)), ((), ()))


def _params(sem, vmem=VMEM_LIMIT):
    return pltpu.CompilerParams(dimension_semantics=sem, vmem_limit_bytes=vmem)


def _rmsnorm(x, g):
    return x * lax.rsqrt(jnp.mean(x * x, axis=-1, keepdims=True) + EPS) * g


def _gelu(x):
    return 0.5 * x * (1.0 + lax.erf(x * (2.0 ** -0.5)))


def _lam(lq1, lk1, lq2, lk2, lam_init):
    a = jnp.sum(lq1 * lk1, axis=-1, keepdims=True)
    b = jnp.sum(lq2 * lk2, axis=-1, keepdims=True)
    return jnp.exp(a) - jnp.exp(b) + lam_init


def _mixer_in_body(x_ref, g_ref, w_ref, cos_ref, sin_ref, q_ref, k_ref, v_ref, u_ref):
    xn = _rmsnorm(x_ref[...], g_ref[...])
    z = jnp.dot(xn.astype(BF16), w_ref[...], preferred_element_type=F32)
    cos = cos_ref[...]
    sin = sin_ref[...]
    lane = lax.broadcasted_iota(I32, (1, LANES), 1)
    first_half = (lane & 32) == 0

    def rope(t):
        partner = jnp.where(first_half, pltpu.roll(t, 96, 1), pltpu.roll(t, 32, 1))
        return t * cos + partner * sin

    for h in range(ATT_HEADS):
        lo = h * LANES
        q_ref[:, lo:lo + LANES] = rope(z[:, lo:lo + LANES])
        k_ref[:, lo:lo + LANES] = rope(z[:, D_ATT + lo:D_ATT + lo + LANES])
    v_ref[...] = z[:, 2 * D_ATT:3 * D_ATT]
    a = z[:, 3 * D_ATT:3 * D_ATT + D_CONV]
    gate = z[:, 3 * D_ATT + D_CONV:]
    u_ref[...] = a * jax.nn.sigmoid(gate)


def _mixer_in_t_body(x_ref, g_ref, w_ref, wt_ref, cos_ref, sin_ref, cost_ref, sint_ref,
                     qt_ref, k_ref, kb_ref, v_ref, vt_ref, u_ref):
    xn = _rmsnorm(x_ref[...], g_ref[...]).astype(BF16)
    z = jnp.dot(xn, w_ref[...], preferred_element_type=F32)
    zt = lax.dot_general(wt_ref[...], xn, _NT, preferred_element_type=F32)
    cos = cos_ref[...]
    sin = sin_ref[...]
    lane = lax.broadcasted_iota(I32, (1, LANES), 1)
    first_half = (lane & 32) == 0
    for h in range(ATT_HEADS):
        t = z[:, h * LANES:(h + 1) * LANES]
        partner = jnp.where(first_half, pltpu.roll(t, 96, 1), pltpu.roll(t, 32, 1))
        kh = t * cos + partner * sin
        k_ref[:, h * LANES:(h + 1) * LANES] = kh
        kb_ref[:, h * LANES:(h + 1) * LANES] = kh.astype(BF16)
    v_ref[...] = z[:, D_ATT:2 * D_ATT]
    u_ref[...] = z[:, 2 * D_ATT:2 * D_ATT + D_CONV] * jax.nn.sigmoid(z[:, 2 * D_ATT + D_CONV:])
    ct = cost_ref[...]
    st = sint_ref[...]
    half = ATT_DH // 2
    qscale = (ATT_DH ** -0.5) * math.log2(math.e)
    for blk in range(2 * ATT_HEADS):
        r0 = blk * ATT_DH
        x1 = zt[r0:r0 + half]
        x2 = zt[r0 + half:r0 + ATT_DH]
        qt_ref[r0:r0 + half, :] = ((x1 * ct - x2 * st) * qscale).astype(BF16)
        qt_ref[r0 + half:r0 + ATT_DH, :] = ((x2 * ct + x1 * st) * qscale).astype(BF16)
    vt_ref[...] = zt[D_ATT:].astype(BF16)


def _mixer_in_t(x2d, g, w_bf16, wt_bf16, tabs, *, b, tm, interpret=False):
    n, d = x2d.shape
    s = n // b
    nt = s // tm
    cos, sin, cost, sint = tabs
    tab = pl.BlockSpec((tm, LANES), lambda i: (i % nt, 0))
    tabt = pl.BlockSpec((ATT_DH // 2, tm), lambda i: (0, i % nt))
    nat = pl.BlockSpec((tm, D_ATT), lambda i: (i, 0))
    fmaj = pl.BlockSpec((None, D_ATT, tm), lambda i: (i // nt, 0, i % nt))
    nat_f32 = jax.ShapeDtypeStruct((n, D_ATT), F32)
    fmaj_b16 = jax.ShapeDtypeStruct((b, D_ATT, s), BF16)
    return pl.pallas_call(
        _mixer_in_t_body,
        out_shape=[fmaj_b16, nat_f32, jax.ShapeDtypeStruct((n, D_ATT), BF16), nat_f32, fmaj_b16, nat_f32],
        grid=(n // tm,),
        in_specs=[pl.BlockSpec((tm, d), lambda i: (i, 0)),
                  pl.BlockSpec((1, d), lambda i: (0, 0)),
                  pl.BlockSpec(w_bf16.shape, lambda i: (0, 0)),
                  pl.BlockSpec(wt_bf16.shape, lambda i: (0, 0)),
                  tab, tab, tabt, tabt],
        out_specs=[fmaj, nat, nat, nat, fmaj, nat],
        compiler_params=_params(("parallel",)),
        name="mixer_in_t", interpret=interpret,
    )(x2d, g, w_bf16, wt_bf16, cos, sin, cost, sint)


def _rope_tables(pos):
    inv = ROPE_THETA ** (-jnp.arange(0, ATT_DH, 2, dtype=F32) / ATT_DH)
    ang = pos.astype(F32)[:, None] * inv[None, :]
    cos = jnp.tile(jnp.cos(ang), (1, 4))
    sin = jnp.sin(ang)
    sin4 = jnp.concatenate([-sin, sin, -sin, sin], axis=1)
    return cos, sin4, jnp.cos(ang).T, sin.T


def _mixer_in(x2d, g, w_bf16, cos, sin, *, tm, interpret=False):
    n, d = x2d.shape
    d_in = w_bf16.shape[1]
    nt = cos.shape[0] // tm
    tab = pl.BlockSpec((tm, LANES), lambda i: (i % nt, 0))
    out = pl.BlockSpec((tm, D_ATT), lambda i: (i, 0))
    return pl.pallas_call(
        _mixer_in_body,
        out_shape=[jax.ShapeDtypeStruct((n, D_ATT), F32)] * 4,
        grid=(n // tm,),
        in_specs=[pl.BlockSpec((tm, d), lambda i: (i, 0)),
                  pl.BlockSpec((1, d), lambda i: (0, 0)),
                  pl.BlockSpec((d, d_in), lambda i: (0, 0)),
                  tab, tab],
        out_specs=[out] * 4,
        compiler_params=_params(("parallel",)),
        name="mixer_in", interpret=interpret,
    )(x2d, g, w_bf16, cos, sin)


def _attn_prompt_body(lq1, lk1, lq2, lk2, qt_ref, k_ref, vt_ref, o_ref, *, tq, lam_init):
    qi = pl.program_id(2)
    lam = _lam(lq1[...], lk1[...], lq2[...], lk2[...], lam_init)
    heads = qt_ref.shape[0] // LANES
    feat = lax.broadcasted_iota(I32, (LANES, tq), 0)
    zero = jnp.zeros((), BF16)
    qq = []
    for g in range(heads):
        qt = qt_ref[g * LANES:(g + 1) * LANES, :]
        qq.append(jnp.concatenate([jnp.where(feat < ATT_DH, qt, zero),
                                   jnp.where(feat >= ATT_DH, qt, zero)], axis=1))

    def step(j, carry, masked):
        start = pl.multiple_of(j * tq, tq)
        out = []
        scores = [jnp.dot(k_ref[pl.ds(start, tq), g * LANES:(g + 1) * LANES], qq[g],
                          preferred_element_type=F32) for g in range(heads)]
        for g in range(heads):
            m, l, acc = carry[g]
            vj = vt_ref[g * LANES:(g + 1) * LANES, pl.ds(start, tq)]
            s = scores[g]
            if masked:
                kpos = lax.broadcasted_iota(I32, s.shape, 0)
                qpos = lax.broadcasted_iota(I32, s.shape, 1) & (tq - 1)
                s = jnp.where(kpos <= qpos, s, -jnp.inf)
            m_new = jnp.maximum(m, jnp.max(s, axis=0, keepdims=True))
            alpha = jnp.exp2(m - m_new)
            p = jnp.exp2(s - m_new)
            l = l * alpha + jnp.sum(p, axis=0, keepdims=True)
            acc = acc * alpha + jnp.dot(vj, p.astype(BF16), preferred_element_type=F32)
            out.append((m_new, l, acc))
        return tuple(out)

    init = tuple((jnp.full((1, 2 * tq), -jnp.inf, F32), jnp.zeros((1, 2 * tq), F32),
                  jnp.zeros((ATT_DV, 2 * tq), F32)) for _ in range(heads))
    carry = lax.fori_loop(0, qi, lambda j, c: step(j, c, False), init)
    carry = step(qi, carry, True)
    for g in range(heads):
        _, l, acc = carry[g]
        o = acc / l
        o_ref[:, g * LANES:(g + 1) * LANES] = (o[:, :tq] - lam * o[:, tq:]).T


def _attn_prompt(lams, qt, kb, vt, *, tq, heads, lam_init, interpret=False):
    b, s, _ = kb.shape
    assert tq & (tq - 1) == 0 and ATT_HEADS % heads == 0
    lam_spec = pl.BlockSpec((1, ATT_DH), lambda bi, h, i: (0, 0))
    w = heads * LANES
    return pl.pallas_call(
        functools.partial(_attn_prompt_body, tq=tq, lam_init=lam_init),
        out_shape=jax.ShapeDtypeStruct((b, s, D_ATT), F32),
        grid=(b, ATT_HEADS // heads, s // tq),
        in_specs=[lam_spec] * 4 + [pl.BlockSpec((None, w, tq), lambda bi, h, i: (bi, h, i)),
                                   pl.BlockSpec((None, s, w), lambda bi, h, i: (bi, 0, h)),
                                   pl.BlockSpec((None, w, s), lambda bi, h, i: (bi, h, 0))],
        out_specs=pl.BlockSpec((None, tq, w), lambda bi, h, i: (bi, i, h)),
        compiler_params=_params(("parallel", "parallel", "arbitrary")),
        name="attn_prompt", interpret=interpret,
    )(*lams, qt, kb, vt)


def _attn_sample_body(pt_ref, lq1, lk1, lq2, lk2, q_ref, kn_ref, vn_ref, *rest, pp, t, lam_init):
    k_refs = rest[:pp]
    v_refs = rest[pp:2 * pp]
    o_ref = rest[2 * pp]
    qm_ref, m_ref, l_ref, acc_ref = rest[2 * pp + 1:]
    g = pl.program_id(1)
    rows = ATT_HEADS * 2 * t
    head_shift = (2 * t).bit_length() - 1

    @pl.when(g == 0)
    def _():
        q = q_ref[...] * (ATT_DH ** -0.5)
        lane = lax.broadcasted_iota(I32, (t, LANES), 1)
        blocks = []
        for h in range(ATT_HEADS):
            qh = q[:, h * LANES:(h + 1) * LANES]
            blocks.append(jnp.where(lane < ATT_DH, qh, 0.0))
            blocks.append(jnp.where(lane >= ATT_DH, qh, 0.0))
        qm_ref[...] = jnp.concatenate(blocks, axis=0).astype(BF16)
        m_ref[...] = jnp.full(m_ref.shape, -jnp.inf, F32)
        l_ref[...] = jnp.zeros(l_ref.shape, F32)
        acc_ref[...] = jnp.zeros(acc_ref.shape, F32)

    def online(s_list, v_list):
        m_prev = m_ref[...]
        m_new = m_prev
        for s in s_list:
            m_new = jnp.maximum(m_new, jnp.max(s, axis=-1, keepdims=True))
        alpha = jnp.exp(m_prev - m_new)
        l = l_ref[...] * alpha
        acc = acc_ref[...] * alpha
        for s, v in zip(s_list, v_list):
            p = jnp.exp(s - m_new)
            l = l + jnp.sum(p, axis=-1, keepdims=True)
            acc = acc + jnp.dot(p.astype(BF16), v, preferred_element_type=F32)
        l_ref[...] = l
        acc_ref[...] = acc
        m_ref[...] = m_new

    qm = qm_ref[...]
    nrow = k_refs[0].shape[0]
    same_head = ((lax.broadcasted_iota(I32, (rows, nrow), 0) >> head_shift)
                 == (lax.broadcasted_iota(I32, (rows, nrow), 1) & (ATT_HEADS - 1)))
    online([jnp.where(same_head,
                      lax.dot_general(qm, k_refs[j][...].astype(BF16), _NT, preferred_element_type=F32),
                      -jnp.inf) for j in range(pp)],
           [v_refs[j][...].astype(BF16) for j in range(pp)])

    @pl.when(g == pl.num_programs(1) - 1)
    def _():
        s2 = lax.dot_general(qm, kn_ref[...].astype(BF16), _NT, preferred_element_type=F32)
        r = lax.broadcasted_iota(I32, s2.shape, 0)
        c = lax.broadcasted_iota(I32, s2.shape, 1)
        ok = ((r >> head_shift) == (c & (ATT_HEADS - 1))) & ((c >> 2) <= (r & (t - 1)))
        online([jnp.where(ok, s2, -jnp.inf)], [vn_ref[...].astype(BF16)])
        lam = _lam(lq1[...], lk1[...], lq2[...], lk2[...], lam_init)
        o = acc_ref[...] / l_ref[...]
        for h in range(ATT_HEADS):
            oh = o[h * 2 * t:(h + 1) * 2 * t]
            o_ref[:, h * LANES:(h + 1) * LANES] = oh[:t] - lam * oh[t:]


def _attn_sample(page_table, lams, q, k_new, v_new, cache_k, cache_v, li, *, pp, lam_init, interpret=False):
    bd, t, _ = q.shape
    n_pages = page_table.shape[1]
    prow = cache_k.shape[2]
    assert n_pages % pp == 0 and t & (t - 1) == 0 and ATT_HEADS == 4
    rows = ATT_HEADS * 2 * t
    lam_spec = pl.BlockSpec((1, ATT_DH), lambda b, g, pt: (0, 0))
    tok_spec = pl.BlockSpec((None, t, D_ATT), lambda b, g, pt: (b, 0, 0))
    new_spec = pl.BlockSpec((None, t * ATT_HEADS, LANES), lambda b, g, pt: (b, 0, 0))

    def page_spec(j):
        return pl.BlockSpec((None, None, prow, LANES), lambda b, g, pt: (li, pt[b, g * pp + j], 0, 0))

    grid_spec = pltpu.PrefetchScalarGridSpec(
        num_scalar_prefetch=1,
        grid=(bd, n_pages // pp),
        in_specs=[lam_spec] * 4 + [tok_spec, new_spec, new_spec] + [page_spec(j) for j in range(pp)] * 2,
        out_specs=tok_spec,
        scratch_shapes=[pltpu.VMEM((rows, LANES), BF16), pltpu.VMEM((rows, 1), F32),
                        pltpu.VMEM((rows, 1), F32), pltpu.VMEM((rows, LANES), F32)],
    )
    return pl.pallas_call(
        functools.partial(_attn_sample_body, pp=pp, t=t, lam_init=lam_init),
        out_shape=jax.ShapeDtypeStruct((bd, t, D_ATT), F32),
        grid_spec=grid_spec,
        compiler_params=_params(("parallel", "arbitrary")),
        name="attn_sample", interpret=interpret,
    )(page_table, *lams, q, k_new, v_new, *([cache_k] * pp), *([cache_v] * pp))


def _mixer_out_body(x_ref, att_ref, u_ref, hist_ref, gsub_ref, cw_ref, cb_ref, lng_ref, lnb_ref,
                    wout_ref, o_ref, ucat_ref, ush_ref, conv_ref, *, tb, t, zero_first_hist, att_gain):
    hist = hist_ref[...]
    if zero_first_hist:
        hist = jnp.where(pl.program_id(1) == 0, 0.0, hist)
    ucat_ref[:, 0:CONV_HIST, :] = hist
    ucat_ref[:, CONV_HIST:CONV_HIST + t, :] = u_ref[...]
    rc = min(t, 64)
    off = CONV_HIST - (CONV_W - 1)
    span = t + CONV_HIST - 8
    for r in range(1, 8):
        ush_ref[r - 1, :, 0:span, :] = ucat_ref[:, r:r + span, :]
    for b in range(tb):
        for r0 in range(0, t, rc):
            acc = jnp.broadcast_to(cb_ref[...], (rc, D_CONV))
            for j in range(CONV_W):
                r = (j + off) % 8
                base = r0 + j + off - r
                src = ucat_ref[b, base:base + rc, :] if r == 0 else ush_ref[r - 1, b, base:base + rc, :]
                acc = acc + cw_ref[j:j + 1, :] * src
            conv_ref[b * t + r0:b * t + r0 + rc, :] = acc
    c = conv_ref[...]
    mu = jnp.mean(c, axis=-1, keepdims=True)
    cc = c - mu
    var = jnp.mean(cc * cc, axis=-1, keepdims=True)
    cn = cc * lax.rsqrt(var + EPS) * lng_ref[...] + lnb_ref[...]
    cact = cn * jax.nn.sigmoid(cn)
    m = tb * t
    att = att_ref[...].reshape(m, D_ATT)
    gsub = gsub_ref[...] * att_gain
    a = jnp.concatenate(
        [_rmsnorm(att[:, h * LANES:(h + 1) * LANES], gsub) for h in range(ATT_HEADS)], axis=1)
    y = jnp.dot(a.astype(BF16), wout_ref[0:D_ATT, :], preferred_element_type=F32)
    y = y + jnp.dot(cact.astype(BF16), wout_ref[D_ATT:, :], preferred_element_type=F32)
    o_ref[...] = (y + x_ref[...].reshape(m, -1)).reshape(o_ref.shape)


def _mixer_out(x, att, u, hist, weights, *, tb, t, hist_from_u, att_gain, interpret=False):
    b, tt, d = x.shape
    gsub, cw, cb, lng, lnb, wout = weights
    nb = t // CONV_HIST
    if hist_from_u:
        hist_spec = pl.BlockSpec((tb, CONV_HIST, D_CONV),
                                 lambda bi, i: (bi, jnp.maximum(i * nb - 1, 0), 0))
    else:
        hist_spec = pl.BlockSpec((tb, CONV_HIST, D_CONV), lambda bi, i: (bi, 0, 0))
    row = lambda w: pl.BlockSpec((1, w), lambda bi, i: (0, 0))
    tile = lambda w: pl.BlockSpec((tb, t, w), lambda bi, i: (bi, i, 0))
    return pl.pallas_call(
        functools.partial(_mixer_out_body, tb=tb, t=t, zero_first_hist=hist_from_u, att_gain=att_gain),
        out_shape=jax.ShapeDtypeStruct(x.shape, F32),
        grid=(b // tb, tt // t),
        in_specs=[tile(d), tile(D_ATT), tile(D_CONV), hist_spec, row(LANES),
                  pl.BlockSpec((CONV_W, D_CONV), lambda bi, i: (0, 0)),
                  row(D_CONV), row(D_CONV), row(D_CONV),
                  pl.BlockSpec((d, d), lambda bi, i: (0, 0))],
        out_specs=tile(d),
        scratch_shapes=[pltpu.VMEM((tb, CONV_HIST + t, D_CONV), F32),
                        pltpu.VMEM((7, tb, CONV_HIST + t - 8, D_CONV), F32),
                        pltpu.VMEM((tb * t, D_CONV), F32)],
        compiler_params=_params(("parallel", "arbitrary")),
        name="mixer_out", interpret=interpret,
    )(x, att, u, hist, gsub, cw, cb, lng, lnb, wout)


def _mem_kv_body(m_ref, g_ref, wk_ref, wv_ref, k_ref, v_ref):
    mn = _rmsnorm(m_ref[...], g_ref[...]).astype(BF16)
    k_ref[...] = jnp.dot(mn, wk_ref[...], preferred_element_type=F32)
    v_ref[...] = jnp.dot(mn, wv_ref[...], preferred_element_type=F32)


def _mem_kv(mem2d, g, wk, wv, *, tm, interpret=False):
    n, d = mem2d.shape
    tile = pl.BlockSpec((tm, d), lambda i: (i, 0))
    full = pl.BlockSpec((d, d), lambda i: (0, 0))
    return pl.pallas_call(
        _mem_kv_body,
        out_shape=[jax.ShapeDtypeStruct((n, d), F32)] * 2,
        grid=(n // tm,),
        in_specs=[tile, pl.BlockSpec((1, d), lambda i: (0, 0)), full, full],
        out_specs=[tile, tile],
        compiler_params=_params(("parallel",)),
        name="mem_kv", interpret=interpret,
    )(mem2d, g, wk, wv)


def _mem_attn_body(x_ref, g_ref, wq_ref, wo_ref, *rest, tb, t, per_head):
    mk_ref, mv_ref, o_ref = rest[0], rest[1], rest[-1]
    m = tb * t
    d = x_ref.shape[-1]
    dh = d // MEM_HEADS
    x = x_ref[...].reshape(m, d)
    hn = _rmsnorm(x, g_ref[...]).astype(BF16)
    q = (jnp.dot(hn, wq_ref[...], preferred_element_type=F32) * (dh ** -0.5)).astype(BF16)
    outs = []
    for b in range(tb):
        heads = []
        for h in range(MEM_HEADS):
            qh = q[b * t:(b + 1) * t, h * dh:(h + 1) * dh]
            if per_head:
                kh = mk_ref[b, :, h, :].astype(BF16)
                vh = mv_ref[b, :, h, :].astype(BF16)
            else:
                kh = mk_ref[b, :, h * dh:(h + 1) * dh].astype(BF16)
                vh = mv_ref[b, :, h * dh:(h + 1) * dh].astype(BF16)
            s = lax.dot_general(qh, kh, _NT, preferred_element_type=F32)
            e = jnp.exp(s - jnp.max(s, axis=-1, keepdims=True))
            p = e / jnp.sum(e, axis=-1, keepdims=True)
            heads.append(jnp.dot(p.astype(BF16), vh, preferred_element_type=F32))
        outs.append(jnp.concatenate(heads, axis=1))
    o = jnp.concatenate(outs, axis=0) if tb > 1 else outs[0]
    y = jnp.dot(o.astype(BF16), wo_ref[...], preferred_element_type=F32) + x
    o_ref[...] = y


def _mem_attn(x, mk, mv, g, wq, wo, *, tb, t, n_rows, row0=0, into=None, interpret=False):
    b, tt, d = x.shape
    mt = mk.shape[1]
    m = tb * t
    nt = tt // t
    assert row0 % m == 0
    tile = pl.BlockSpec((tb, t, d), lambda bi, i: (bi, i, 0))
    per_head = mk.ndim == 4
    if per_head:
        mem = [pl.BlockSpec((tb, mt) + mk.shape[2:], lambda bi, i: (bi, 0, 0, 0))]
    else:
        mem = [pl.BlockSpec((tb, mt, d), lambda bi, i: (bi, 0, 0))]
    full = pl.BlockSpec((d, d), lambda bi, i: (0, 0))
    in_specs = [tile, pl.BlockSpec((1, d), lambda bi, i: (0, 0)), full, full] + mem + mem
    args = [x, g, wq, wo] + [mk] * len(mem) + [mv] * len(mem)
    aliases = {}
    if into is not None:
        in_specs.append(pl.BlockSpec(memory_space=pl.ANY))
        args.append(into)
        aliases = {len(args) - 1: 0}
    return pl.pallas_call(
        functools.partial(_mem_attn_body, tb=tb, t=t, per_head=per_head),
        out_shape=jax.ShapeDtypeStruct((n_rows, d), F32),
        grid=(b // tb, nt),
        in_specs=in_specs,
        out_specs=pl.BlockSpec((m, d), lambda bi, i: (row0 // m + bi * nt + i, 0)),
        input_output_aliases=aliases,
        compiler_params=_params(("parallel", "arbitrary")),
        name="mem_attn", interpret=interpret,
    )(*args)


def _topk16(s, want_rank):
    r = s.shape[0]
    iota = lax.broadcasted_iota(I32, s.shape, 0)
    vals, idxs = [], []
    rank = jnp.full(s.shape, PEER_TOPK, I32) if want_rank else None
    for k in range(PEER_TOPK):
        m = jnp.max(s, axis=0, keepdims=True)
        idx = jnp.min(jnp.where(s == m, iota, r), axis=0, keepdims=True)
        hit = iota == idx
        if want_rank:
            rank = jnp.where(hit, k, rank)
        s = jnp.where(hit, -jnp.inf, s)
        vals.append(m)
        idxs.append(idx)
    return jnp.concatenate(vals, axis=0), jnp.concatenate(idxs, axis=0), rank


def _route_exact(s1, s2):
    sv1, _, r1 = _topk16(s1, True)
    sv2, _, r2 = _topk16(s2, True)
    cand = jnp.concatenate([sv1[k:k + 1] + sv2 for k in range(PEER_TOPK)], axis=0)
    tv, tp, _ = _topk16(cand, False)
    k1 = tp >> 4
    n1 = jnp.zeros(s1.shape, F32)
    for k in range(PEER_TOPK):
        n1 = n1 + jnp.where(r1 == k1[k:k + 1], 1.0, 0.0)
    z = jnp.sum(jnp.exp(tv - tv[0:1]), axis=0, keepdims=True)
    return n1, r2, z


def _topk16_distinct(s, want_rank):
    vals = []
    rank = jnp.full(s.shape, PEER_TOPK, I32) if want_rank else None
    for k in range(PEER_TOPK):
        m = jnp.max(s, axis=0, keepdims=True)
        hit = s == m
        if want_rank:
            rank = jnp.where(hit, k, rank)
        s = jnp.where(hit, -jnp.inf, s)
        vals.append(m)
    return vals, rank, s


_CAND_GROUPS = (
    ((0, 0, 0, 8),),
    ((0, 0, 8, 8),),
    ((1, 0, 0, 8),),
    ((2, 0, 0, 5), (4, 5, 0, 3)),
    ((3, 0, 0, 4), (5, 4, 0, 2), (6, 6, 0, 2)),
    ((7, 0, 0, 2),) + tuple((8 + i, 2 + i, 0, 1) for i in range(6)),
    ((14, 0, 0, 1), (15, 1, 0, 1)),
)


def _route_distinct(s1, s2):
    t = s1.shape[1]
    sv1, r1, _ = _topk16_distinct(s1, True)
    sv2, r2, _ = _topk16_distinct(s2, True)
    sv2_lo = jnp.concatenate(sv2[:8], axis=0)
    sv2_hi = jnp.concatenate(sv2[8:], axis=0)
    row = lax.broadcasted_iota(I32, (8, t), 0)
    groups = []
    for pieces in _CAND_GROUPS:
        val = jnp.full((8, t), -jnp.inf, F32)
        for k1, off, k2, nk in pieces:
            src = sv2_hi if k2 else sv2_lo
            if off:
                src = pltpu.roll(src, off, 0)
            val = jnp.where((row >= off) & (row < off + nk), sv1[k1] + src, val)
        groups.append(val)
    cand = jnp.concatenate(groups, axis=0)
    tv, _, left = _topk16_distinct(cand, False)
    won = jnp.where(left != cand, 1.0, 0.0)
    cnt = [jnp.zeros((1, t), F32) for _ in range(PEER_TOPK)]
    for gi, pieces in enumerate(_CAND_GROUPS):
        w = won[gi * 8:(gi + 1) * 8]
        for k1, off, _, nk in pieces:
            cnt[k1] = cnt[k1] + jnp.sum(jnp.where((row >= off) & (row < off + nk), w, 0.0),
                                        axis=0, keepdims=True)
    n1 = jnp.zeros(s1.shape, F32)
    for k in range(PEER_TOPK):
        n1 = jnp.where(r1 == k, cnt[k], n1)
    z = jnp.zeros((1, t), F32)
    for k in range(PEER_TOPK):
        z = z + jnp.exp(tv[k] - tv[0])
    picked = lambda r: jnp.sum(jnp.where(r < PEER_TOPK, 1.0, 0.0), axis=0, keepdims=True)
    total = cnt[0]
    for k in range(1, PEER_TOPK):
        total = total + cnt[k]
    tied = (picked(r1) != PEER_TOPK) | (picked(r2) != PEER_TOPK) | (total != PEER_TOPK)
    return n1, r2, z, sv1[0], sv2[0], tied


def _peer_route_body(x_ref, g_ref, wpq_ref, sk_ref, xn_ref, n1_ref, e1_ref, r2_ref, e2_ref, q_ref):
    xnt = _rmsnorm(x_ref[...], g_ref[...]).T.astype(BF16)
    xn_ref[...] = xnt
    q_ref[...] = jnp.dot(wpq_ref[...], xnt, preferred_element_type=F32).astype(BF16)

    group = 4

    def heads(hg, carry):
        def scores(hc):
            r0 = pl.multiple_of(hc * LANES, LANES)
            return jnp.dot(sk_ref[hc], q_ref[pl.ds(r0, LANES), :], preferred_element_type=F32)

        done = []
        any_tied = None
        for hh in range(group):
            h = hg * group + hh
            s1 = scores(h * 2)
            s2 = scores(h * 2 + 1)
            n1, r2, z, top1, top2, tied = _route_distinct(s1, s2)
            ex2 = jnp.exp(s2 - top2)
            n1_ref[h] = n1
            e1_ref[h] = jnp.exp(s1 - top1)
            r2_ref[h] = r2.astype(BF16)
            e2_ref[h] = (ex2 / z).astype(BF16)
            done.append((h, s1, s2, ex2))
            any_tied = tied if any_tied is None else any_tied | tied

        @pl.when(jnp.max(jnp.where(any_tied, 1.0, 0.0)) > 0.5)
        def _():
            for h, s1, s2, ex2 in done:
                n1x, r2x, zx = _route_exact(s1, s2)
                n1_ref[h] = n1x
                r2_ref[h] = r2x.astype(BF16)
                e2_ref[h] = (ex2 / zx).astype(BF16)

        return carry

    lax.fori_loop(0, PEER_HEADS // group, heads, 0)


def _peer_route(x2d, g, wpq, sk, *, tm, interpret=False):
    n, d = x2d.shape
    dq = wpq.shape[0]
    route = pl.BlockSpec((PEER_HEADS, N_KEYS, tm), lambda i: (0, 0, i))
    route_f32 = jax.ShapeDtypeStruct((PEER_HEADS, N_KEYS, n), F32)
    route_b16 = jax.ShapeDtypeStruct((PEER_HEADS, N_KEYS, n), BF16)
    return pl.pallas_call(
        _peer_route_body,
        out_shape=[jax.ShapeDtypeStruct((d, n), BF16), route_f32, route_f32, route_b16, route_b16],
        grid=(n // tm,),
        in_specs=[pl.BlockSpec((tm, d), lambda i: (i, 0)),
                  pl.BlockSpec((1, d), lambda i: (0, 0)),
                  pl.BlockSpec((dq, d), lambda i: (0, 0)),
                  pl.BlockSpec(sk.shape, lambda i: (0, 0, 0))],
        out_specs=[pl.BlockSpec((d, tm), lambda i: (0, i))] + [route] * 4,
        scratch_shapes=[pltpu.VMEM((dq, tm), BF16)],
        compiler_params=_params(("parallel",)),
        name="peer_route", interpret=interpret,
    )(x2d, g, wpq, sk)


def _row_bf16(row):
    slab = jnp.broadcast_to(row, (16, row.shape[1])).astype(BF16)
    return jnp.concatenate([slab] * (N_KEYS // 16), axis=0)


def _peer_dense_body(x_ref, xn_ref, n1_ref, e1_ref, r2_ref, e2_ref, u_ref, vt_ref, gf_ref, y_ref, y2_ref,
                     acc_ref, w_ref, *, ti, tiles_first):
    j = pl.program_id(1)

    @pl.when(j == 0)
    def _():
        acc_ref[...] = jnp.zeros(acc_ref.shape, F32)

    xn = xn_ref[...]
    ht = [jnp.dot(u_ref[il * N_KEYS:(il + 1) * N_KEYS, :], xn, preferred_element_type=F32)
          for il in range(ti)]
    out = None
    kc = 2 * N_KEYS
    assert ti % 8 == 0
    for il in range(ti):
        slab = pl.ds(pl.multiple_of(j * ti + (il // 8) * 8, 8), 8)
        sub = il % 8
        rows = slice(il * N_KEYS, (il + 1) * N_KEYS)
        gate = None
        for h in range(PEER_HEADS):
            n1 = _row_bf16(n1_ref[h, slab, :][sub:sub + 1])
            e1 = _row_bf16(e1_ref[h, slab, :][sub:sub + 1])
            term = jnp.where(r2_ref[h] < n1, e2_ref[h], jnp.zeros((), BF16)) * e1
            gate = term if gate is None else gate + term
        w_ref[rows, :] = _gelu(ht[il]).astype(BF16) * gate
        if (il + 1) * N_KEYS % kc == 0:
            k0 = (il + 1) * N_KEYS - kc
            part = jnp.dot(vt_ref[:, k0:k0 + kc], w_ref[k0:k0 + kc, :], preferred_element_type=F32)
            out = part if out is None else out + part
    acc_ref[...] += out

    last = j == pl.num_programs(1) - 1
    first_group = pl.program_id(0) < tiles_first

    @pl.when(last & first_group)
    def _():
        y_ref[...] = _rmsnorm(x_ref[...] + acc_ref[...].T, gf_ref[...])

    @pl.when(last & jnp.logical_not(first_group))
    def _():
        y2_ref[...] = _rmsnorm(x_ref[...] + acc_ref[...].T, gf_ref[...])


def _peer_dense(x2d, xn, n1, e1, r2, e2, u_bf16, vt_bf16, g_final, *, n_first, tm, ti, interpret=False):
    n, d = x2d.shape
    ne = u_bf16.shape[0]
    te = ti * N_KEYS
    assert n_first % tm == 0 and (n - n_first) % tm == 0 and n > n_first
    tf = n_first // tm
    tok = pl.BlockSpec((tm, d), lambda i, j: (i, 0))
    route = pl.BlockSpec((PEER_HEADS, N_KEYS, tm), lambda i, j: (0, 0, i))
    return pl.pallas_call(
        functools.partial(_peer_dense_body, ti=ti, tiles_first=tf),
        out_shape=[jax.ShapeDtypeStruct((n_first, d), F32), jax.ShapeDtypeStruct((n - n_first, d), F32)],
        grid=(n // tm, ne // te),
        in_specs=[tok, pl.BlockSpec((d, tm), lambda i, j: (0, i)), route, route, route, route,
                  pl.BlockSpec((te, d), lambda i, j: (j, 0)),
                  pl.BlockSpec((d, te), lambda i, j: (0, j)),
                  pl.BlockSpec((1, d), lambda i, j: (0, 0))],
        out_specs=[pl.BlockSpec((tm, d), lambda i, j: (jnp.minimum(i, tf - 1), 0)),
                   pl.BlockSpec((tm, d), lambda i, j: (jnp.maximum(i - tf, 0), 0))],
        scratch_shapes=[pltpu.VMEM((d, tm), F32), pltpu.VMEM((te, tm), BF16)],
        compiler_params=_params(("parallel", "arbitrary")),
        name="peer_dense", interpret=interpret,
    )(x2d, xn, n1, e1, r2, e2, u_bf16, vt_bf16, g_final)


def _pick(n, pref):
    t = min(n, pref)
    while n % t:
        t //= 2
    return t


def _forward(x_prompt, x_sample, cache_k, cache_v, state_conv, cache_mem_k, cache_mem_v, page_table,
             mem_prompt, g_norm_mix, w_in, lam_q1, lam_k1, lam_q2, lam_k2, g_subln, conv_w, conv_b,
             conv_ln_g, conv_ln_b, w_out, g_norm_mem, g_mem_src, w_mq, w_mk, w_mv, w_mo, g_norm_peer,
             w_pq, sub_keys, u_tab, v_tab, g_final, *, interpret=False):
    depth = w_in.shape[0]
    assert depth == 1, "single layer: the final norm is fused into the PEER kernel"
    li = 0
    b, s, d = x_prompt.shape
    bd, ts, _ = x_sample.shape
    past = page_table.shape[1] * cache_k.shape[2]
    n_pool, page = cache_k.shape[1], cache_k.shape[2]
    mt = mem_prompt.shape[1]
    kw = dict(interpret=interpret)
    row = lambda a: a.reshape(1, -1)

    lam_init = 0.8 - 0.6 * math.exp(-0.3 * li)
    lams = (row(lam_q1[li]), row(lam_k1[li]), row(lam_q2[li]), row(lam_k2[li]))
    w_in_b = w_in[li].astype(BF16)
    mix_w = (row(g_subln[li]), conv_w[li], row(conv_b[li]), row(conv_ln_g[li]), row(conv_ln_b[li]),
             w_out[li].astype(BF16))

    tm_p = _pick(s, 512)
    w_nat = w_in_b[:, D_ATT:]
    w_t = jnp.concatenate([w_in_b[:, :D_ATT], w_in_b[:, 2 * D_ATT:3 * D_ATT]], axis=1).T
    qtp, kp, kbp, vp, vtp, up = _mixer_in_t(x_prompt.reshape(b * s, d), row(g_norm_mix[li]), w_nat, w_t,
                                            _rope_tables(jnp.arange(s, dtype=I32)), b=b, tm=tm_p, **kw)
    ns = bd * ts
    tm_s = _pick(ns, 512)
    cos_s, sin_s, _, _ = _rope_tables(past + jnp.arange(ts, dtype=I32))
    cos_s = jnp.tile(cos_s, (tm_s // ts, 1))
    sin_s = jnp.tile(sin_s, (tm_s // ts, 1))
    qs, ks, vs, us = _mixer_in(x_sample.reshape(ns, d), row(g_norm_mix[li]), w_in_b, cos_s, sin_s,
                               tm=tm_s, **kw)
    r3 = lambda a, bb: a.reshape(bb, -1, a.shape[-1])
    kp, kbp, vp, up = (r3(a, b) for a in (kp, kbp, vp, up))
    qs, ks, vs, us = (r3(a, bd) for a in (qs, ks, vs, us))

    att_p = _attn_prompt(lams, qtp, kbp, vtp, tq=_pick(s, 512), heads=4, lam_init=lam_init, **kw)
    flat = lambda c: c.reshape(c.shape[0], n_pool, page * ATT_HEADS, LANES)
    att_s = _attn_sample(page_table, lams, qs, ks.reshape(bd, ts * ATT_HEADS, LANES),
                         vs.reshape(bd, ts * ATT_HEADS, LANES), flat(cache_k), flat(cache_v), li,
                         pp=_pick(page_table.shape[1], 16), lam_init=lam_init, **kw)

    xp = _mixer_out(x_prompt, att_p, up, up, mix_w, tb=1, t=tm_p, hist_from_u=True,
                    att_gain=1.0 - lam_init, **kw)
    state = state_conv[li]
    hist_s = jnp.pad(state, ((0, 0), (CONV_HIST - state.shape[1], 0), (0, 0)))
    xs = _mixer_out(x_sample, att_s, us, hist_s, mix_w, tb=_pick(bd, 16), t=ts, hist_from_u=False,
                    att_gain=1.0 - lam_init, **kw)
    conv_p = up[:, s - (CONV_W - 1):]
    conv_s = jnp.concatenate([state, us], axis=1)[:, -(CONV_W - 1):]

    mk, mv = _mem_kv(mem_prompt.reshape(b * mt, d), row(g_mem_src[li]), w_mk[li].astype(BF16),
                     w_mv[li].astype(BF16), tm=_pick(b * mt, 512), **kw)
    mk = mk.reshape(b, mt, d)
    mv = mv.reshape(b, mt, d)
    wq_b, wo_b = w_mq[li].astype(BF16), w_mo[li].astype(BF16)
    n_all = b * s + ns
    x_all = _mem_attn(xp, mk, mv, row(g_norm_mem[li]), wq_b, wo_b, tb=1, t=tm_p, n_rows=n_all, **kw)
    x_all = _mem_attn(xs, cache_mem_k[li], cache_mem_v[li], row(g_norm_mem[li]), wq_b, wo_b,
                      tb=_pick(bd, 8), t=ts, n_rows=n_all, row0=b * s, into=x_all, **kw)

    sk = sub_keys[li].reshape(PEER_HEADS * 2, N_KEYS, -1).astype(BF16)
    xn, n1, e1, r2, e2 = _peer_route(x_all, row(g_norm_peer[li]), w_pq[li].T.astype(BF16), sk,
                                     tm=_pick(n_all, 256), **kw)
    y_prompt, y_sample = _peer_dense(x_all, xn, n1, e1, r2, e2, u_tab[li].astype(BF16),
                                     v_tab[li].T.astype(BF16), row(g_final), n_first=b * s,
                                     tm=_pick(math.gcd(b * s, ns), 512), ti=_pick(N_KEYS, 16), **kw)
    y_prompt = y_prompt.reshape(b, s, d)
    y_sample = y_sample.reshape(bd, ts, d)

    k4 = lambda a: a.reshape(1, a.shape[0], a.shape[1], ATT_HEADS, -1)
    return (y_prompt, y_sample, k4(kp), k4(vp), conv_p[None],
            mk.reshape(1, b, mt, MEM_HEADS, -1), mv.reshape(1, b, mt, MEM_HEADS, -1),
            k4(ks), k4(vs), conv_s[None])


def kernel(x_prompt, x_sample, cache_k, cache_v, state_conv, cache_mem_k, cache_mem_v, page_table, mem_prompt, g_norm_mix, w_in, lam_q1, lam_k1, lam_q2, lam_k2, g_subln, conv_w, conv_b, conv_ln_g, conv_ln_b, w_out, g_norm_mem, g_mem_src, w_mq, w_mk, w_mv, w_mo, g_norm_peer, w_pq, sub_keys, u_tab, v_tab, g_final):
    return _forward(x_prompt, x_sample, cache_k, cache_v, state_conv, cache_mem_k, cache_mem_v, page_table,
                    mem_prompt, g_norm_mix, w_in, lam_q1, lam_k1, lam_q2, lam_k2, g_subln, conv_w, conv_b,
                    conv_ln_g, conv_ln_b, w_out, g_norm_mem, g_mem_src, w_mq, w_mk, w_mv, w_mo, g_norm_peer,
                    w_pq, sub_keys, u_tab, v_tab, g_final)
```

```python
import functools
import math

import jax
import jax.numpy as jnp
from jax import lax
from jax.experimental import pallas as pl
from jax.experimental.pallas import tpu as pltpu

F32 = jnp.float32
BF16 = jnp.bfloat16
I32 = jnp.int32

EPS = 1e-6
ROPE_THETA = 10000.0
ATT_HEADS = 4
ATT_DH = 64
ATT_DV = 128
D_ATT = 512
D_CONV = 512
CONV_W = 31
CONV_HIST = 32
MEM_HEADS = 4
PEER_HEADS = 8
N_KEYS = 128
PEER_TOPK = 16
LANES = 128
VMEM_LIMIT = 56 * 1024 * 1024

_NT = (((1,), (1,)), ((), ()))


def _params(sem, vmem=VMEM_LIMIT):
    return pltpu.CompilerParams(dimension_semantics=sem, vmem_limit_bytes=vmem)


def _rmsnorm(x, g):
    return x * lax.rsqrt(jnp.mean(x * x, axis=-1, keepdims=True) + EPS) * g


def _gelu(x):
    return 0.5 * x * (1.0 + lax.erf(x * (2.0 ** -0.5)))


def _lam(lq1, lk1, lq2, lk2, lam_init):
    a = jnp.sum(lq1 * lk1, axis=-1, keepdims=True)
    b = jnp.sum(lq2 * lk2, axis=-1, keepdims=True)
    return jnp.exp(a) - jnp.exp(b) + lam_init


def _mixer_in_body(x_ref, g_ref, w_ref, cos_ref, sin_ref, q_ref, k_ref, v_ref, u_ref):
    xn = _rmsnorm(x_ref[...], g_ref[...])
    z = jnp.dot(xn.astype(BF16), w_ref[...], preferred_element_type=F32)
    cos = cos_ref[...]
    sin = sin_ref[...]
    lane = lax.broadcasted_iota(I32, (1, LANES), 1)
    first_half = (lane & 32) == 0

    def rope(t):
        partner = jnp.where(first_half, pltpu.roll(t, 96, 1), pltpu.roll(t, 32, 1))
        return t * cos + partner * sin

    for h in range(ATT_HEADS):
        lo = h * LANES
        q_ref[:, lo:lo + LANES] = rope(z[:, lo:lo + LANES])
        k_ref[:, lo:lo + LANES] = rope(z[:, D_ATT + lo:D_ATT + lo + LANES])
    v_ref[...] = z[:, 2 * D_ATT:3 * D_ATT]
    a = z[:, 3 * D_ATT:3 * D_ATT + D_CONV]
    gate = z[:, 3 * D_ATT + D_CONV:]
    u_ref[...] = a * jax.nn.sigmoid(gate)


def _mixer_in_t_body(x_ref, g_ref, w_ref, wt_ref, cos_ref, sin_ref, cost_ref, sint_ref,
                     qt_ref, k_ref, kb_ref, v_ref, vt_ref, u_ref):
    xn = _rmsnorm(x_ref[...], g_ref[...]).astype(BF16)
    z = jnp.dot(xn, w_ref[...], preferred_element_type=F32)
    zt = lax.dot_general(wt_ref[...], xn, _NT, preferred_element_type=F32)
    cos = cos_ref[...]
    sin = sin_ref[...]
    lane = lax.broadcasted_iota(I32, (1, LANES), 1)
    first_half = (lane & 32) == 0
    for h in range(ATT_HEADS):
        t = z[:, h * LANES:(h + 1) * LANES]
        partner = jnp.where(first_half, pltpu.roll(t, 96, 1), pltpu.roll(t, 32, 1))
        kh = t * cos + partner * sin
        k_ref[:, h * LANES:(h + 1) * LANES] = kh
        kb_ref[:, h * LANES:(h + 1) * LANES] = kh.astype(BF16)
    v_ref[...] = z[:, D_ATT:2 * D_ATT]
    u_ref[...] = z[:, 2 * D_ATT:2 * D_ATT + D_CONV] * jax.nn.sigmoid(z[:, 2 * D_ATT + D_CONV:])
    ct = cost_ref[...]
    st = sint_ref[...]
    half = ATT_DH // 2
    qscale = (ATT_DH ** -0.5) * math.log2(math.e)
    for blk in range(2 * ATT_HEADS):
        r0 = blk * ATT_DH
        x1 = zt[r0:r0 + half]
        x2 = zt[r0 + half:r0 + ATT_DH]
        qt_ref[r0:r0 + half, :] = ((x1 * ct - x2 * st) * qscale).astype(BF16)
        qt_ref[r0 + half:r0 + ATT_DH, :] = ((x2 * ct + x1 * st) * qscale).astype(BF16)
    vt_ref[...] = zt[D_ATT:].astype(BF16)


def _mixer_in_t(x2d, g, w_bf16, wt_bf16, tabs, *, b, tm, interpret=False):
    n, d = x2d.shape
    s = n // b
    nt = s // tm
    cos, sin, cost, sint = tabs
    tab = pl.BlockSpec((tm, LANES), lambda i: (i % nt, 0))
    tabt = pl.BlockSpec((ATT_DH // 2, tm), lambda i: (0, i % nt))
    nat = pl.BlockSpec((tm, D_ATT), lambda i: (i, 0))
    fmaj = pl.BlockSpec((None, D_ATT, tm), lambda i: (i // nt, 0, i % nt))
    nat_f32 = jax.ShapeDtypeStruct((n, D_ATT), F32)
    fmaj_b16 = jax.ShapeDtypeStruct((b, D_ATT, s), BF16)
    return pl.pallas_call(
        _mixer_in_t_body,
        out_shape=[fmaj_b16, nat_f32, jax.ShapeDtypeStruct((n, D_ATT), BF16), nat_f32, fmaj_b16, nat_f32],
        grid=(n // tm,),
        in_specs=[pl.BlockSpec((tm, d), lambda i: (i, 0)),
                  pl.BlockSpec((1, d), lambda i: (0, 0)),
                  pl.BlockSpec(w_bf16.shape, lambda i: (0, 0)),
                  pl.BlockSpec(wt_bf16.shape, lambda i: (0, 0)),
                  tab, tab, tabt, tabt],
        out_specs=[fmaj, nat, nat, nat, fmaj, nat],
        compiler_params=_params(("parallel",)),
        name="mixer_in_t", interpret=interpret,
    )(x2d, g, w_bf16, wt_bf16, cos, sin, cost, sint)


def _rope_tables(pos):
    inv = ROPE_THETA ** (-jnp.arange(0, ATT_DH, 2, dtype=F32) / ATT_DH)
    ang = pos.astype(F32)[:, None] * inv[None, :]
    cos = jnp.tile(jnp.cos(ang), (1, 4))
    sin = jnp.sin(ang)
    sin4 = jnp.concatenate([-sin, sin, -sin, sin], axis=1)
    return cos, sin4, jnp.cos(ang).T, sin.T


def _mixer_in(x2d, g, w_bf16, cos, sin, *, tm, interpret=False):
    n, d = x2d.shape
    d_in = w_bf16.shape[1]
    nt = cos.shape[0] // tm
    tab = pl.BlockSpec((tm, LANES), lambda i: (i % nt, 0))
    out = pl.BlockSpec((tm, D_ATT), lambda i: (i, 0))
    return pl.pallas_call(
        _mixer_in_body,
        out_shape=[jax.ShapeDtypeStruct((n, D_ATT), F32)] * 4,
        grid=(n // tm,),
        in_specs=[pl.BlockSpec((tm, d), lambda i: (i, 0)),
                  pl.BlockSpec((1, d), lambda i: (0, 0)),
                  pl.BlockSpec((d, d_in), lambda i: (0, 0)),
                  tab, tab],
        out_specs=[out] * 4,
        compiler_params=_params(("parallel",)),
        name="mixer_in", interpret=interpret,
    )(x2d, g, w_bf16, cos, sin)


def _attn_prompt_body(lq1, lk1, lq2, lk2, qt_ref, k_ref, vt_ref, o_ref, *, tq, lam_init):
    qi = pl.program_id(2)
    lam = _lam(lq1[...], lk1[...], lq2[...], lk2[...], lam_init)
    heads = qt_ref.shape[0] // LANES
    feat = lax.broadcasted_iota(I32, (LANES, tq), 0)
    zero = jnp.zeros((), BF16)
    qq = []
    for g in range(heads):
        qt = qt_ref[g * LANES:(g + 1) * LANES, :]
        qq.append(jnp.concatenate([jnp.where(feat < ATT_DH, qt, zero),
                                   jnp.where(feat >= ATT_DH, qt, zero)], axis=1))

    def step(j, carry, masked):
        start = pl.multiple_of(j * tq, tq)
        out = []
        scores = [jnp.dot(k_ref[pl.ds(start, tq), g * LANES:(g + 1) * LANES], qq[g],
                          preferred_element_type=F32) for g in range(heads)]
        for g in range(heads):
            m, l, acc = carry[g]
            vj = vt_ref[g * LANES:(g + 1) * LANES, pl.ds(start, tq)]
            s = scores[g]
            if masked:
                kpos = lax.broadcasted_iota(I32, s.shape, 0)
                qpos = lax.broadcasted_iota(I32, s.shape, 1) & (tq - 1)
                s = jnp.where(kpos <= qpos, s, -jnp.inf)
            m_new = jnp.maximum(m, jnp.max(s, axis=0, keepdims=True))
            alpha = jnp.exp2(m - m_new)
            p = jnp.exp2(s - m_new)
            l = l * alpha + jnp.sum(p, axis=0, keepdims=True)
            acc = acc * alpha + jnp.dot(vj, p.astype(BF16), preferred_element_type=F32)
            out.append((m_new, l, acc))
        return tuple(out)

    init = tuple((jnp.full((1, 2 * tq), -jnp.inf, F32), jnp.zeros((1, 2 * tq), F32),
                  jnp.zeros((ATT_DV, 2 * tq), F32)) for _ in range(heads))
    carry = lax.fori_loop(0, qi, lambda j, c: step(j, c, False), init)
    carry = step(qi, carry, True)
    for g in range(heads):
        _, l, acc = carry[g]
        o = acc / l
        o_ref[:, g * LANES:(g + 1) * LANES] = (o[:, :tq] - lam * o[:, tq:]).T


def _attn_prompt(lams, qt, kb, vt, *, tq, heads, lam_init, interpret=False):
    b, s, _ = kb.shape
    assert tq & (tq - 1) == 0 and ATT_HEADS % heads == 0
    lam_spec = pl.BlockSpec((1, ATT_DH), lambda bi, h, i: (0, 0))
    w = heads * LANES
    return pl.pallas_call(
        functools.partial(_attn_prompt_body, tq=tq, lam_init=lam_init),
        out_shape=jax.ShapeDtypeStruct((b, s, D_ATT), F32),
        grid=(b, ATT_HEADS // heads, s // tq),
        in_specs=[lam_spec] * 4 + [pl.BlockSpec((None, w, tq), lambda bi, h, i: (bi, h, i)),
                                   pl.BlockSpec((None, s, w), lambda bi, h, i: (bi, 0, h)),
                                   pl.BlockSpec((None, w, s), lambda bi, h, i: (bi, h, 0))],
        out_specs=pl.BlockSpec((None, tq, w), lambda bi, h, i: (bi, i, h)),
        compiler_params=_params(("parallel", "parallel", "arbitrary")),
        name="attn_prompt", interpret=interpret,
    )(*lams, qt, kb, vt)


def _attn_sample_body(pt_ref, lq1, lk1, lq2, lk2, q_ref, kn_ref, vn_ref, *rest, pp, t, lam_init):
    k_refs = rest[:pp]
    v_refs = rest[pp:2 * pp]
    o_ref = rest[2 * pp]
    qm_ref, m_ref, l_ref, acc_ref = rest[2 * pp + 1:]
    g = pl.program_id(1)
    rows = ATT_HEADS * 2 * t
    head_shift = (2 * t).bit_length() - 1

    @pl.when(g == 0)
    def _():
        q = q_ref[...] * (ATT_DH ** -0.5)
        lane = lax.broadcasted_iota(I32, (t, LANES), 1)
        blocks = []
        for h in range(ATT_HEADS):
            qh = q[:, h * LANES:(h + 1) * LANES]
            blocks.append(jnp.where(lane < ATT_DH, qh, 0.0))
            blocks.append(jnp.where(lane >= ATT_DH, qh, 0.0))
        qm_ref[...] = jnp.concatenate(blocks, axis=0).astype(BF16)
        m_ref[...] = jnp.full(m_ref.shape, -jnp.inf, F32)
        l_ref[...] = jnp.zeros(l_ref.shape, F32)
        acc_ref[...] = jnp.zeros(acc_ref.shape, F32)

    def online(s_list, v_list):
        m_prev = m_ref[...]
        m_new = m_prev
        for s in s_list:
            m_new = jnp.maximum(m_new, jnp.max(s, axis=-1, keepdims=True))
        alpha = jnp.exp(m_prev - m_new)
        l = l_ref[...] * alpha
        acc = acc_ref[...] * alpha
        for s, v in zip(s_list, v_list):
            p = jnp.exp(s - m_new)
            l = l + jnp.sum(p, axis=-1, keepdims=True)
            acc = acc + jnp.dot(p.astype(BF16), v, preferred_element_type=F32)
        l_ref[...] = l
        acc_ref[...] = acc
        m_ref[...] = m_new

    qm = qm_ref[...]
    nrow = k_refs[0].shape[0]
    same_head = ((lax.broadcasted_iota(I32, (rows, nrow), 0) >> head_shift)
                 == (lax.broadcasted_iota(I32, (rows, nrow), 1) & (ATT_HEADS - 1)))
    online([jnp.where(same_head,
                      lax.dot_general(qm, k_refs[j][...].astype(BF16), _NT, preferred_element_type=F32),
                      -jnp.inf) for j in range(pp)],
           [v_refs[j][...].astype(BF16) for j in range(pp)])

    @pl.when(g == pl.num_programs(1) - 1)
    def _():
        s2 = lax.dot_general(qm, kn_ref[...].astype(BF16), _NT, preferred_element_type=F32)
        r = lax.broadcasted_iota(I32, s2.shape, 0)
        c = lax.broadcasted_iota(I32, s2.shape, 1)
        ok = ((r >> head_shift) == (c & (ATT_HEADS - 1))) & ((c >> 2) <= (r & (t - 1)))
        online([jnp.where(ok, s2, -jnp.inf)], [vn_ref[...].astype(BF16)])
        lam = _lam(lq1[...], lk1[...], lq2[...], lk2[...], lam_init)
        o = acc_ref[...] / l_ref[...]
        for h in range(ATT_HEADS):
            oh = o[h * 2 * t:(h + 1) * 2 * t]
            o_ref[:, h * LANES:(h + 1) * LANES] = oh[:t] - lam * oh[t:]


def _attn_sample(page_table, lams, q, k_new, v_new, cache_k, cache_v, li, *, pp, lam_init, interpret=False):
    bd, t, _ = q.shape
    n_pages = page_table.shape[1]
    prow = cache_k.shape[2]
    assert n_pages % pp == 0 and t & (t - 1) == 0 and ATT_HEADS == 4
    rows = ATT_HEADS * 2 * t
    lam_spec = pl.BlockSpec((1, ATT_DH), lambda b, g, pt: (0, 0))
    tok_spec = pl.BlockSpec((None, t, D_ATT), lambda b, g, pt: (b, 0, 0))
    new_spec = pl.BlockSpec((None, t * ATT_HEADS, LANES), lambda b, g, pt: (b, 0, 0))

    def page_spec(j):
        return pl.BlockSpec((None, None, prow, LANES), lambda b, g, pt: (li, pt[b, g * pp + j], 0, 0))

    grid_spec = pltpu.PrefetchScalarGridSpec(
        num_scalar_prefetch=1,
        grid=(bd, n_pages // pp),
        in_specs=[lam_spec] * 4 + [tok_spec, new_spec, new_spec] + [page_spec(j) for j in range(pp)] * 2,
        out_specs=tok_spec,
        scratch_shapes=[pltpu.VMEM((rows, LANES), BF16), pltpu.VMEM((rows, 1), F32),
                        pltpu.VMEM((rows, 1), F32), pltpu.VMEM((rows, LANES), F32)],
    )
    return pl.pallas_call(
        functools.partial(_attn_sample_body, pp=pp, t=t, lam_init=lam_init),
        out_shape=jax.ShapeDtypeStruct((bd, t, D_ATT), F32),
        grid_spec=grid_spec,
        compiler_params=_params(("parallel", "arbitrary")),
        name="attn_sample", interpret=interpret,
    )(page_table, *lams, q, k_new, v_new, *([cache_k] * pp), *([cache_v] * pp))


def _mixer_out_body(x_ref, att_ref, u_ref, hist_ref, gsub_ref, cw_ref, cb_ref, lng_ref, lnb_ref,
                    wout_ref, o_ref, ucat_ref, ush_ref, conv_ref, *, tb, t, zero_first_hist, att_gain):
    hist = hist_ref[...]
    if zero_first_hist:
        hist = jnp.where(pl.program_id(1) == 0, 0.0, hist)
    ucat_ref[:, 0:CONV_HIST, :] = hist
    ucat_ref[:, CONV_HIST:CONV_HIST + t, :] = u_ref[...]
    rc = min(t, 64)
    off = CONV_HIST - (CONV_W - 1)
    span = t + CONV_HIST - 8
    for r in range(1, 8):
        ush_ref[r - 1, :, 0:span, :] = ucat_ref[:, r:r + span, :]
    for b in range(tb):
        for r0 in range(0, t, rc):
            acc = jnp.broadcast_to(cb_ref[...], (rc, D_CONV))
            for j in range(CONV_W):
                r = (j + off) % 8
                base = r0 + j + off - r
                src = ucat_ref[b, base:base + rc, :] if r == 0 else ush_ref[r - 1, b, base:base + rc, :]
                acc = acc + cw_ref[j:j + 1, :] * src
            conv_ref[b * t + r0:b * t + r0 + rc, :] = acc
    c = conv_ref[...]
    mu = jnp.mean(c, axis=-1, keepdims=True)
    cc = c - mu
    var = jnp.mean(cc * cc, axis=-1, keepdims=True)
    cn = cc * lax.rsqrt(var + EPS) * lng_ref[...] + lnb_ref[...]
    cact = cn * jax.nn.sigmoid(cn)
    m = tb * t
    att = att_ref[...].reshape(m, D_ATT)
    gsub = gsub_ref[...] * att_gain
    a = jnp.concatenate(
        [_rmsnorm(att[:, h * LANES:(h + 1) * LANES], gsub) for h in range(ATT_HEADS)], axis=1)
    y = jnp.dot(a.astype(BF16), wout_ref[0:D_ATT, :], preferred_element_type=F32)
    y = y + jnp.dot(cact.astype(BF16), wout_ref[D_ATT:, :], preferred_element_type=F32)
    o_ref[...] = (y + x_ref[...].reshape(m, -1)).reshape(o_ref.shape)


def _mixer_out(x, att, u, hist, weights, *, tb, t, hist_from_u, att_gain, interpret=False):
    b, tt, d = x.shape
    gsub, cw, cb, lng, lnb, wout = weights
    nb = t // CONV_HIST
    if hist_from_u:
        hist_spec = pl.BlockSpec((tb, CONV_HIST, D_CONV),
                                 lambda bi, i: (bi, jnp.maximum(i * nb - 1, 0), 0))
    else:
        hist_spec = pl.BlockSpec((tb, CONV_HIST, D_CONV), lambda bi, i: (bi, 0, 0))
    row = lambda w: pl.BlockSpec((1, w), lambda bi, i: (0, 0))
    tile = lambda w: pl.BlockSpec((tb, t, w), lambda bi, i: (bi, i, 0))
    return pl.pallas_call(
        functools.partial(_mixer_out_body, tb=tb, t=t, zero_first_hist=hist_from_u, att_gain=att_gain),
        out_shape=jax.ShapeDtypeStruct(x.shape, F32),
        grid=(b // tb, tt // t),
        in_specs=[tile(d), tile(D_ATT), tile(D_CONV), hist_spec, row(LANES),
                  pl.BlockSpec((CONV_W, D_CONV), lambda bi, i: (0, 0)),
                  row(D_CONV), row(D_CONV), row(D_CONV),
                  pl.BlockSpec((d, d), lambda bi, i: (0, 0))],
        out_specs=tile(d),
        scratch_shapes=[pltpu.VMEM((tb, CONV_HIST + t, D_CONV), F32),
                        pltpu.VMEM((7, tb, CONV_HIST + t - 8, D_CONV), F32),
                        pltpu.VMEM((tb * t, D_CONV), F32)],
        compiler_params=_params(("parallel", "arbitrary")),
        name="mixer_out", interpret=interpret,
    )(x, att, u, hist, gsub, cw, cb, lng, lnb, wout)


def _mem_kv_body(m_ref, g_ref, wk_ref, wv_ref, k_ref, v_ref):
    mn = _rmsnorm(m_ref[...], g_ref[...]).astype(BF16)
    k_ref[...] = jnp.dot(mn, wk_ref[...], preferred_element_type=F32)
    v_ref[...] = jnp.dot(mn, wv_ref[...], preferred_element_type=F32)


def _mem_kv(mem2d, g, wk, wv, *, tm, interpret=False):
    n, d = mem2d.shape
    tile = pl.BlockSpec((tm, d), lambda i: (i, 0))
    full = pl.BlockSpec((d, d), lambda i: (0, 0))
    return pl.pallas_call(
        _mem_kv_body,
        out_shape=[jax.ShapeDtypeStruct((n, d), F32)] * 2,
        grid=(n // tm,),
        in_specs=[tile, pl.BlockSpec((1, d), lambda i: (0, 0)), full, full],
        out_specs=[tile, tile],
        compiler_params=_params(("parallel",)),
        name="mem_kv", interpret=interpret,
    )(mem2d, g, wk, wv)


def _mem_attn_body(x_ref, g_ref, wq_ref, wo_ref, *rest, tb, t, per_head):
    mk_ref, mv_ref, o_ref = rest[0], rest[1], rest[-1]
    m = tb * t
    d = x_ref.shape[-1]
    dh = d // MEM_HEADS
    x = x_ref[...].reshape(m, d)
    hn = _rmsnorm(x, g_ref[...]).astype(BF16)
    q = (jnp.dot(hn, wq_ref[...], preferred_element_type=F32) * (dh ** -0.5)).astype(BF16)
    outs = []
    for b in range(tb):
        heads = []
        for h in range(MEM_HEADS):
            qh = q[b * t:(b + 1) * t, h * dh:(h + 1) * dh]
            if per_head:
                kh = mk_ref[b, :, h, :]
                vh = mv_ref[b, :, h, :]
                qh = qh.astype(F32)
            else:
                kh = mk_ref[b, :, h * dh:(h + 1) * dh].astype(BF16)
                vh = mv_ref[b, :, h * dh:(h + 1) * dh].astype(BF16)
            s = lax.dot_general(qh, kh, _NT, preferred_element_type=F32)
            e = jnp.exp(s - jnp.max(s, axis=-1, keepdims=True))
            p = e / jnp.sum(e, axis=-1, keepdims=True)
            heads.append(jnp.dot(p.astype(vh.dtype), vh, preferred_element_type=F32))
        outs.append(jnp.concatenate(heads, axis=1))
    o = jnp.concatenate(outs, axis=0) if tb > 1 else outs[0]
    y = jnp.dot(o.astype(BF16), wo_ref[...], preferred_element_type=F32) + x
    o_ref[...] = y


def _mem_attn(x, mk, mv, g, wq, wo, *, tb, t, n_rows, row0=0, into=None, interpret=False):
    b, tt, d = x.shape
    mt = mk.shape[1]
    m = tb * t
    nt = tt // t
    assert row0 % m == 0
    tile = pl.BlockSpec((tb, t, d), lambda bi, i: (bi, i, 0))
    per_head = mk.ndim == 4
    if per_head:
        mem = [pl.BlockSpec((tb, mt) + mk.shape[2:], lambda bi, i: (bi, 0, 0, 0))]
    else:
        mem = [pl.BlockSpec((tb, mt, d), lambda bi, i: (bi, 0, 0))]
    full = pl.BlockSpec((d, d), lambda bi, i: (0, 0))
    in_specs = [tile, pl.BlockSpec((1, d), lambda bi, i: (0, 0)), full, full] + mem + mem
    args = [x, g, wq, wo] + [mk] * len(mem) + [mv] * len(mem)
    aliases = {}
    if into is not None:
        in_specs.append(pl.BlockSpec(memory_space=pl.ANY))
        args.append(into)
        aliases = {len(args) - 1: 0}
    return pl.pallas_call(
        functools.partial(_mem_attn_body, tb=tb, t=t, per_head=per_head),
        out_shape=jax.ShapeDtypeStruct((n_rows, d), F32),
        grid=(b // tb, nt),
        in_specs=in_specs,
        out_specs=pl.BlockSpec((m, d), lambda bi, i: (row0 // m + bi * nt + i, 0)),
        input_output_aliases=aliases,
        compiler_params=_params(("parallel", "arbitrary")),
        name="mem_attn", interpret=interpret,
    )(*args)


def _topk16(s, want_rank):
    r = s.shape[0]
    iota = lax.broadcasted_iota(I32, s.shape, 0)
    vals, idxs = [], []
    rank = jnp.full(s.shape, PEER_TOPK, I32) if want_rank else None
    for k in range(PEER_TOPK):
        m = jnp.max(s, axis=0, keepdims=True)
        idx = jnp.min(jnp.where(s == m, iota, r), axis=0, keepdims=True)
        hit = iota == idx
        if want_rank:
            rank = jnp.where(hit, k, rank)
        s = jnp.where(hit, -jnp.inf, s)
        vals.append(m)
        idxs.append(idx)
    return jnp.concatenate(vals, axis=0), jnp.concatenate(idxs, axis=0), rank


def _route_exact(s1, s2):
    sv1, _, r1 = _topk16(s1, True)
    sv2, _, r2 = _topk16(s2, True)
    cand = jnp.concatenate([sv1[k:k + 1] + sv2 for k in range(PEER_TOPK)], axis=0)
    tv, tp, _ = _topk16(cand, False)
    k1 = tp >> 4
    n1 = jnp.zeros(s1.shape, F32)
    for k in range(PEER_TOPK):
        n1 = n1 + jnp.where(r1 == k1[k:k + 1], 1.0, 0.0)
    z = jnp.sum(jnp.exp(tv - tv[0:1]), axis=0, keepdims=True)
    return n1, r2, z


def _topk16_distinct(s, want_rank):
    vals = []
    rank = jnp.full(s.shape, PEER_TOPK, I32) if want_rank else None
    for k in range(PEER_TOPK):
        m = jnp.max(s, axis=0, keepdims=True)
        hit = s == m
        if want_rank:
            rank = jnp.where(hit, k, rank)
        s = jnp.where(hit, -jnp.inf, s)
        vals.append(m)
    return vals, rank, s


_CAND_GROUPS = (
    ((0, 0, 0, 8),),
    ((0, 0, 8, 8),),
    ((1, 0, 0, 8),),
    ((2, 0, 0, 5), (4, 5, 0, 3)),
    ((3, 0, 0, 4), (5, 4, 0, 2), (6, 6, 0, 2)),
    ((7, 0, 0, 2),) + tuple((8 + i, 2 + i, 0, 1) for i in range(6)),
    ((14, 0, 0, 1), (15, 1, 0, 1)),
)


def _route_distinct(s1, s2):
    t = s1.shape[1]
    sv1, r1, _ = _topk16_distinct(s1, True)
    sv2, r2, _ = _topk16_distinct(s2, True)
    sv2_lo = jnp.concatenate(sv2[:8], axis=0)
    sv2_hi = jnp.concatenate(sv2[8:], axis=0)
    row = lax.broadcasted_iota(I32, (8, t), 0)
    groups = []
    for pieces in _CAND_GROUPS:
        val = jnp.full((8, t), -jnp.inf, F32)
        for k1, off, k2, nk in pieces:
            src = sv2_hi if k2 else sv2_lo
            if off:
                src = pltpu.roll(src, off, 0)
            val = jnp.where((row >= off) & (row < off + nk), sv1[k1] + src, val)
        groups.append(val)
    cand = jnp.concatenate(groups, axis=0)
    tv, _, left = _topk16_distinct(cand, False)
    won = jnp.where(left != cand, 1.0, 0.0)
    cnt = [jnp.zeros((1, t), F32) for _ in range(PEER_TOPK)]
    for gi, pieces in enumerate(_CAND_GROUPS):
        w = won[gi * 8:(gi + 1) * 8]
        for k1, off, _, nk in pieces:
            cnt[k1] = cnt[k1] + jnp.sum(jnp.where((row >= off) & (row < off + nk), w, 0.0),
                                        axis=0, keepdims=True)
    n1 = jnp.zeros(s1.shape, F32)
    for k in range(PEER_TOPK):
        n1 = jnp.where(r1 == k, cnt[k], n1)
    z = jnp.zeros((1, t), F32)
    for k in range(PEER_TOPK):
        z = z + jnp.exp(tv[k] - tv[0])
    picked = lambda r: jnp.sum(jnp.where(r < PEER_TOPK, 1.0, 0.0), axis=0, keepdims=True)
    total = cnt[0]
    for k in range(1, PEER_TOPK):
        total = total + cnt[k]
    tied = (picked(r1) != PEER_TOPK) | (picked(r2) != PEER_TOPK) | (total != PEER_TOPK)
    return n1, r2, z, sv1[0], sv2[0], tied


def _peer_route_body(x_ref, g_ref, wpq_ref, sk_ref, xn_ref, n1_ref, e1_ref, r2_ref, e2_ref, q_ref):
    xnt = _rmsnorm(x_ref[...], g_ref[...]).T.astype(BF16)
    xn_ref[...] = xnt
    q_ref[...] = jnp.dot(wpq_ref[...], xnt, preferred_element_type=F32).astype(BF16)

    group = 4

    def heads(hg, carry):
        def scores(hc):
            r0 = pl.multiple_of(hc * LANES, LANES)
            return jnp.dot(sk_ref[hc], q_ref[pl.ds(r0, LANES), :], preferred_element_type=F32)

        done = []
        any_tied = None
        for hh in range(group):
            h = hg * group + hh
            s1 = scores(h * 2)
            s2 = scores(h * 2 + 1)
            n1, r2, z, top1, top2, tied = _route_distinct(s1, s2)
            ex2 = jnp.exp(s2 - top2)
            n1_ref[h] = n1
            e1_ref[h] = jnp.exp(s1 - top1)
            r2_ref[h] = r2.astype(BF16)
            e2_ref[h] = (ex2 / z).astype(BF16)
            done.append((h, s1, s2, ex2))
            any_tied = tied if any_tied is None else any_tied | tied

        @pl.when(jnp.max(jnp.where(any_tied, 1.0, 0.0)) > 0.5)
        def _():
            for h, s1, s2, ex2 in done:
                n1x, r2x, zx = _route_exact(s1, s2)
                n1_ref[h] = n1x
                r2_ref[h] = r2x.astype(BF16)
                e2_ref[h] = (ex2 / zx).astype(BF16)

        return carry

    lax.fori_loop(0, PEER_HEADS // group, heads, 0)


def _peer_route(x2d, g, wpq, sk, *, tm, interpret=False):
    n, d = x2d.shape
    dq = wpq.shape[0]
    route = pl.BlockSpec((PEER_HEADS, N_KEYS, tm), lambda i: (0, 0, i))
    route_f32 = jax.ShapeDtypeStruct((PEER_HEADS, N_KEYS, n), F32)
    route_b16 = jax.ShapeDtypeStruct((PEER_HEADS, N_KEYS, n), BF16)
    return pl.pallas_call(
        _peer_route_body,
        out_shape=[jax.ShapeDtypeStruct((d, n), BF16), route_f32, route_f32, route_b16, route_b16],
        grid=(n // tm,),
        in_specs=[pl.BlockSpec((tm, d), lambda i: (i, 0)),
                  pl.BlockSpec((1, d), lambda i: (0, 0)),
                  pl.BlockSpec((dq, d), lambda i: (0, 0)),
                  pl.BlockSpec(sk.shape, lambda i: (0, 0, 0))],
        out_specs=[pl.BlockSpec((d, tm), lambda i: (0, i))] + [route] * 4,
        scratch_shapes=[pltpu.VMEM((dq, tm), BF16)],
        compiler_params=_params(("parallel",)),
        name="peer_route", interpret=interpret,
    )(x2d, g, wpq, sk)


def _row_bf16(row):
    slab = jnp.broadcast_to(row, (16, row.shape[1])).astype(BF16)
    return jnp.concatenate([slab] * (N_KEYS // 16), axis=0)


def _peer_dense_body(x_ref, xn_ref, n1_ref, e1_ref, r2_ref, e2_ref, u_ref, vt_ref, gf_ref, y_ref, y2_ref,
                     acc_ref, w_ref, *, ti, tiles_first):
    j = pl.program_id(1)

    @pl.when(j == 0)
    def _():
        acc_ref[...] = jnp.zeros(acc_ref.shape, F32)

    xn = xn_ref[...]
    ht = [jnp.dot(u_ref[il * N_KEYS:(il + 1) * N_KEYS, :], xn, preferred_element_type=F32)
          for il in range(ti)]
    out = None
    kc = 2 * N_KEYS
    assert ti % 8 == 0
    for il in range(ti):
        slab = pl.ds(pl.multiple_of(j * ti + (il // 8) * 8, 8), 8)
        sub = il % 8
        rows = slice(il * N_KEYS, (il + 1) * N_KEYS)
        gate = None
        for h in range(PEER_HEADS):
            n1 = _row_bf16(n1_ref[h, slab, :][sub:sub + 1])
            e1 = _row_bf16(e1_ref[h, slab, :][sub:sub + 1])
            term = jnp.where(r2_ref[h] < n1, e2_ref[h], jnp.zeros((), BF16)) * e1
            gate = term if gate is None else gate + term
        w_ref[rows, :] = _gelu(ht[il]).astype(BF16) * gate
        if (il + 1) * N_KEYS % kc == 0:
            k0 = (il + 1) * N_KEYS - kc
            part = jnp.dot(vt_ref[:, k0:k0 + kc], w_ref[k0:k0 + kc, :], preferred_element_type=F32)
            out = part if out is None else out + part
    acc_ref[...] += out

    last = j == pl.num_programs(1) - 1
    first_group = pl.program_id(0) < tiles_first

    @pl.when(last & first_group)
    def _():
        y_ref[...] = _rmsnorm(x_ref[...] + acc_ref[...].T, gf_ref[...])

    @pl.when(last & jnp.logical_not(first_group))
    def _():
        y2_ref[...] = _rmsnorm(x_ref[...] + acc_ref[...].T, gf_ref[...])


def _peer_dense(x2d, xn, n1, e1, r2, e2, u_bf16, vt_bf16, g_final, *, n_first, tm, ti, interpret=False):
    n, d = x2d.shape
    ne = u_bf16.shape[0]
    te = ti * N_KEYS
    assert n_first % tm == 0 and (n - n_first) % tm == 0 and n > n_first
    tf = n_first // tm
    tok = pl.BlockSpec((tm, d), lambda i, j: (i, 0))
    route = pl.BlockSpec((PEER_HEADS, N_KEYS, tm), lambda i, j: (0, 0, i))
    return pl.pallas_call(
        functools.partial(_peer_dense_body, ti=ti, tiles_first=tf),
        out_shape=[jax.ShapeDtypeStruct((n_first, d), F32), jax.ShapeDtypeStruct((n - n_first, d), F32)],
        grid=(n // tm, ne // te),
        in_specs=[tok, pl.BlockSpec((d, tm), lambda i, j: (0, i)), route, route, route, route,
                  pl.BlockSpec((te, d), lambda i, j: (j, 0)),
                  pl.BlockSpec((d, te), lambda i, j: (0, j)),
                  pl.BlockSpec((1, d), lambda i, j: (0, 0))],
        out_specs=[pl.BlockSpec((tm, d), lambda i, j: (jnp.minimum(i, tf - 1), 0)),
                   pl.BlockSpec((tm, d), lambda i, j: (jnp.maximum(i - tf, 0), 0))],
        scratch_shapes=[pltpu.VMEM((d, tm), F32), pltpu.VMEM((te, tm), BF16)],
        compiler_params=_params(("parallel", "arbitrary")),
        name="peer_dense", interpret=interpret,
    )(x2d, xn, n1, e1, r2, e2, u_bf16, vt_bf16, g_final)


def _pick(n, pref):
    t = min(n, pref)
    while n % t:
        t //= 2
    return t


def _forward(x_prompt, x_sample, cache_k, cache_v, state_conv, cache_mem_k, cache_mem_v, page_table,
             mem_prompt, g_norm_mix, w_in, lam_q1, lam_k1, lam_q2, lam_k2, g_subln, conv_w, conv_b,
             conv_ln_g, conv_ln_b, w_out, g_norm_mem, g_mem_src, w_mq, w_mk, w_mv, w_mo, g_norm_peer,
             w_pq, sub_keys, u_tab, v_tab, g_final, *, interpret=False):
    depth = w_in.shape[0]
    assert depth == 1, "single layer: the final norm is fused into the PEER kernel"
    li = 0
    b, s, d = x_prompt.shape
    bd, ts, _ = x_sample.shape
    past = page_table.shape[1] * cache_k.shape[2]
    n_pool, page = cache_k.shape[1], cache_k.shape[2]
    mt = mem_prompt.shape[1]
    kw = dict(interpret=interpret)
    row = lambda a: a.reshape(1, -1)

    lam_init = 0.8 - 0.6 * math.exp(-0.3 * li)
    lams = (row(lam_q1[li]), row(lam_k1[li]), row(lam_q2[li]), row(lam_k2[li]))
    w_in_b = w_in[li].astype(BF16)
    mix_w = (row(g_subln[li]), conv_w[li], row(conv_b[li]), row(conv_ln_g[li]), row(conv_ln_b[li]),
             w_out[li].astype(BF16))

    tm_p = _pick(s, 512)
    w_nat = w_in_b[:, D_ATT:]
    w_t = jnp.concatenate([w_in_b[:, :D_ATT], w_in_b[:, 2 * D_ATT:3 * D_ATT]], axis=1).T
    qtp, kp, kbp, vp, vtp, up = _mixer_in_t(x_prompt.reshape(b * s, d), row(g_norm_mix[li]), w_nat, w_t,
                                            _rope_tables(jnp.arange(s, dtype=I32)), b=b, tm=tm_p, **kw)
    ns = bd * ts
    tm_s = _pick(ns, 512)
    cos_s, sin_s, _, _ = _rope_tables(past + jnp.arange(ts, dtype=I32))
    cos_s = jnp.tile(cos_s, (tm_s // ts, 1))
    sin_s = jnp.tile(sin_s, (tm_s // ts, 1))
    qs, ks, vs, us = _mixer_in(x_sample.reshape(ns, d), row(g_norm_mix[li]), w_in_b, cos_s, sin_s,
                               tm=tm_s, **kw)
    r3 = lambda a, bb: a.reshape(bb, -1, a.shape[-1])
    kp, kbp, vp, up = (r3(a, b) for a in (kp, kbp, vp, up))
    qs, ks, vs, us = (r3(a, bd) for a in (qs, ks, vs, us))

    att_p = _attn_prompt(lams, qtp, kbp, vtp, tq=_pick(s, 512), heads=4, lam_init=lam_init, **kw)
    flat = lambda c: c.reshape(c.shape[0], n_pool, page * ATT_HEADS, LANES)
    att_s = _attn_sample(page_table, lams, qs, ks.reshape(bd, ts * ATT_HEADS, LANES),
                         vs.reshape(bd, ts * ATT_HEADS, LANES), flat(cache_k), flat(cache_v), li,
                         pp=_pick(page_table.shape[1], 16), lam_init=lam_init, **kw)

    xp = _mixer_out(x_prompt, att_p, up, up, mix_w, tb=1, t=tm_p, hist_from_u=True,
                    att_gain=1.0 - lam_init, **kw)
    state = state_conv[li]
    hist_s = jnp.pad(state, ((0, 0), (CONV_HIST - state.shape[1], 0), (0, 0)))
    xs = _mixer_out(x_sample, att_s, us, hist_s, mix_w, tb=_pick(bd, 16), t=ts, hist_from_u=False,
                    att_gain=1.0 - lam_init, **kw)
    conv_p = up[:, s - (CONV_W - 1):]
    conv_s = jnp.concatenate([state, us], axis=1)[:, -(CONV_W - 1):]

    mk, mv = _mem_kv(mem_prompt.reshape(b * mt, d), row(g_mem_src[li]), w_mk[li].astype(BF16),
                     w_mv[li].astype(BF16), tm=_pick(b * mt, 512), **kw)
    mk = mk.reshape(b, mt, d)
    mv = mv.reshape(b, mt, d)
    wq_b, wo_b = w_mq[li].astype(BF16), w_mo[li].astype(BF16)
    n_all = b * s + ns
    x_all = _mem_attn(xp, mk, mv, row(g_norm_mem[li]), wq_b, wo_b, tb=1, t=tm_p, n_rows=n_all, **kw)
    x_all = _mem_attn(xs, cache_mem_k[li], cache_mem_v[li], row(g_norm_mem[li]), wq_b, wo_b,
                      tb=_pick(bd, 8), t=ts, n_rows=n_all, row0=b * s, into=x_all, **kw)

    sk = sub_keys[li].reshape(PEER_HEADS * 2, N_KEYS, -1).astype(BF16)
    xn, n1, e1, r2, e2 = _peer_route(x_all, row(g_norm_peer[li]), w_pq[li].T.astype(BF16), sk,
                                     tm=_pick(n_all, 256), **kw)
    y_prompt, y_sample = _peer_dense(x_all, xn, n1, e1, r2, e2, u_tab[li].astype(BF16),
                                     v_tab[li].T.astype(BF16), row(g_final), n_first=b * s,
                                     tm=_pick(math.gcd(b * s, ns), 512), ti=_pick(N_KEYS, 16), **kw)
    y_prompt = y_prompt.reshape(b, s, d)
    y_sample = y_sample.reshape(bd, ts, d)

    k4 = lambda a: a.reshape(1, a.shape[0], a.shape[1], ATT_HEADS, -1)
    return (y_prompt, y_sample, k4(kp), k4(vp), conv_p[None],
            mk.reshape(1, b, mt, MEM_HEADS, -1), mv.reshape(1, b, mt, MEM_HEADS, -1),
            k4(ks), k4(vs), conv_s[None])


def kernel(x_prompt, x_sample, cache_k, cache_v, state_conv, cache_mem_k, cache_mem_v, page_table, mem_prompt, g_norm_mix, w_in, lam_q1, lam_k1, lam_q2, lam_k2, g_subln, conv_w, conv_b, conv_ln_g, conv_ln_b, w_out, g_norm_mem, g_mem_src, w_mq, w_mk, w_mv, w_mo, g_norm_peer, w_pq, sub_keys, u_tab, v_tab, g_final):
    return _forward(x_prompt, x_sample, cache_k, cache_v, state_conv, cache_mem_k, cache_mem_v, page_table,
                    mem_prompt, g_norm_mix, w_in, lam_q1, lam_k1, lam_q2, lam_k2, g_subln, conv_w, conv_b,
                    conv_ln_g, conv_ln_b, w_out, g_norm_mem, g_mem_src, w_mq, w_mk, w_mv, w_mo, g_norm_peer,
                    w_pq, sub_keys, u_tab, v_tab, g_final)
```

```python
import functools
import math

import jax
import jax.numpy as jnp
from jax import lax
from jax.experimental import pallas as pl
from jax.experimental.pallas import tpu as pltpu

F32 = jnp.float32
BF16 = jnp.bfloat16
I32 = jnp.int32

EPS = 1e-6
ROPE_THETA = 10000.0
ATT_HEADS = 4
ATT_DH = 64
ATT_DV = 128
D_ATT = 512
D_CONV = 512
CONV_W = 31
CONV_HIST = 32
MEM_HEADS = 4
PEER_HEADS = 8
N_KEYS = 128
PEER_TOPK = 16
LANES = 128
VMEM_LIMIT = 56 * 1024 * 1024

_NT = (((1,), (1,)), ((), ()))


def _params(sem, vmem=VMEM_LIMIT):
    return pltpu.CompilerParams(dimension_semantics=sem, vmem_limit_bytes=vmem)


def _rmsnorm(x, g):
    return x * lax.rsqrt(jnp.mean(x * x, axis=-1, keepdims=True) + EPS) * g


def _gelu(x):
    return 0.5 * x * (1.0 + lax.erf(x * (2.0 ** -0.5)))


def _lam(lq1, lk1, lq2, lk2, lam_init):
    a = jnp.sum(lq1 * lk1, axis=-1, keepdims=True)
    b = jnp.sum(lq2 * lk2, axis=-1, keepdims=True)
    return jnp.exp(a) - jnp.exp(b) + lam_init


def _mixer_in_body(x_ref, g_ref, w_ref, cos_ref, sin_ref, q_ref, k_ref, v_ref, u_ref):
    xn = _rmsnorm(x_ref[...], g_ref[...])
    z = jnp.dot(xn.astype(BF16), w_ref[...], preferred_element_type=F32)
    cos = cos_ref[...]
    sin = sin_ref[...]
    lane = lax.broadcasted_iota(I32, (1, LANES), 1)
    first_half = (lane & 32) == 0

    def rope(t):
        partner = jnp.where(first_half, pltpu.roll(t, 96, 1), pltpu.roll(t, 32, 1))
        return t * cos + partner * sin

    for h in range(ATT_HEADS):
        lo = h * LANES
        q_ref[:, lo:lo + LANES] = rope(z[:, lo:lo + LANES])
        k_ref[:, lo:lo + LANES] = rope(z[:, D_ATT + lo:D_ATT + lo + LANES])
    v_ref[...] = z[:, 2 * D_ATT:3 * D_ATT]
    a = z[:, 3 * D_ATT:3 * D_ATT + D_CONV]
    gate = z[:, 3 * D_ATT + D_CONV:]
    u_ref[...] = a * jax.nn.sigmoid(gate)


def _mixer_in_t_body(x_ref, g_ref, w_ref, wt_ref, cos_ref, sin_ref, cost_ref, sint_ref,
                     qt_ref, k_ref, kb_ref, v_ref, vt_ref, u_ref):
    xn = _rmsnorm(x_ref[...], g_ref[...]).astype(BF16)
    z = jnp.dot(xn, w_ref[...], preferred_element_type=F32)
    zt = lax.dot_general(wt_ref[...], xn, _NT, preferred_element_type=F32)
    cos = cos_ref[...]
    sin = sin_ref[...]
    lane = lax.broadcasted_iota(I32, (1, LANES), 1)
    first_half = (lane & 32) == 0
    for h in range(ATT_HEADS):
        t = z[:, h * LANES:(h + 1) * LANES]
        partner = jnp.where(first_half, pltpu.roll(t, 96, 1), pltpu.roll(t, 32, 1))
        kh = t * cos + partner * sin
        k_ref[:, h * LANES:(h + 1) * LANES] = kh
        kb_ref[:, h * LANES:(h + 1) * LANES] = kh.astype(BF16)
    v_ref[...] = z[:, D_ATT:2 * D_ATT]
    u_ref[...] = z[:, 2 * D_ATT:2 * D_ATT + D_CONV] * jax.nn.sigmoid(z[:, 2 * D_ATT + D_CONV:])
    ct = cost_ref[...]
    st = sint_ref[...]
    half = ATT_DH // 2
    qscale = (ATT_DH ** -0.5) * math.log2(math.e)
    for blk in range(2 * ATT_HEADS):
        r0 = blk * ATT_DH
        x1 = zt[r0:r0 + half]
        x2 = zt[r0 + half:r0 + ATT_DH]
        qt_ref[r0:r0 + half, :] = ((x1 * ct - x2 * st) * qscale).astype(BF16)
        qt_ref[r0 + half:r0 + ATT_DH, :] = ((x2 * ct + x1 * st) * qscale).astype(BF16)
    vt_ref[...] = zt[D_ATT:].astype(BF16)


def _mixer_in_t(x2d, g, w_bf16, wt_bf16, tabs, *, b, tm, interpret=False):
    n, d = x2d.shape
    s = n // b
    nt = s // tm
    cos, sin, cost, sint = tabs
    tab = pl.BlockSpec((tm, LANES), lambda i: (i % nt, 0))
    tabt = pl.BlockSpec((ATT_DH // 2, tm), lambda i: (0, i % nt))
    nat = pl.BlockSpec((tm, D_ATT), lambda i: (i, 0))
    fmaj = pl.BlockSpec((None, D_ATT, tm), lambda i: (i // nt, 0, i % nt))
    nat_f32 = jax.ShapeDtypeStruct((n, D_ATT), F32)
    fmaj_b16 = jax.ShapeDtypeStruct((b, D_ATT, s), BF16)
    return pl.pallas_call(
        _mixer_in_t_body,
        out_shape=[fmaj_b16, nat_f32, jax.ShapeDtypeStruct((n, D_ATT), BF16), nat_f32, fmaj_b16, nat_f32],
        grid=(n // tm,),
        in_specs=[pl.BlockSpec((tm, d), lambda i: (i, 0)),
                  pl.BlockSpec((1, d), lambda i: (0, 0)),
                  pl.BlockSpec(w_bf16.shape, lambda i: (0, 0)),
                  pl.BlockSpec(wt_bf16.shape, lambda i: (0, 0)),
                  tab, tab, tabt, tabt],
        out_specs=[fmaj, nat, nat, nat, fmaj, nat],
        compiler_params=_params(("parallel",)),
        name="mixer_in_t", interpret=interpret,
    )(x2d, g, w_bf16, wt_bf16, cos, sin, cost, sint)


def _rope_tables(pos):
    inv = ROPE_THETA ** (-jnp.arange(0, ATT_DH, 2, dtype=F32) / ATT_DH)
    ang = pos.astype(F32)[:, None] * inv[None, :]
    cos = jnp.tile(jnp.cos(ang), (1, 4))
    sin = jnp.sin(ang)
    sin4 = jnp.concatenate([-sin, sin, -sin, sin], axis=1)
    return cos, sin4, jnp.cos(ang).T, sin.T


def _mixer_in(x2d, g, w_bf16, cos, sin, *, tm, interpret=False):
    n, d = x2d.shape
    d_in = w_bf16.shape[1]
    nt = cos.shape[0] // tm
    tab = pl.BlockSpec((tm, LANES), lambda i: (i % nt, 0))
    out = pl.BlockSpec((tm, D_ATT), lambda i: (i, 0))
    return pl.pallas_call(
        _mixer_in_body,
        out_shape=[jax.ShapeDtypeStruct((n, D_ATT), F32)] * 4,
        grid=(n // tm,),
        in_specs=[pl.BlockSpec((tm, d), lambda i: (i, 0)),
                  pl.BlockSpec((1, d), lambda i: (0, 0)),
                  pl.BlockSpec((d, d_in), lambda i: (0, 0)),
                  tab, tab],
        out_specs=[out] * 4,
        compiler_params=_params(("parallel",)),
        name="mixer_in", interpret=interpret,
    )(x2d, g, w_bf16, cos, sin)


def _attn_prompt_body(lq1, lk1, lq2, lk2, qt_ref, k_ref, vt_ref, o_ref, *, tq, lam_init):
    qi = pl.program_id(2)
    lam = _lam(lq1[...], lk1[...], lq2[...], lk2[...], lam_init)
    heads = qt_ref.shape[0] // LANES
    feat = lax.broadcasted_iota(I32, (LANES, tq), 0)
    zero = jnp.zeros((), BF16)
    qq = []
    for g in range(heads):
        qt = qt_ref[g * LANES:(g + 1) * LANES, :]
        qq.append(jnp.concatenate([jnp.where(feat < ATT_DH, qt, zero),
                                   jnp.where(feat >= ATT_DH, qt, zero)], axis=1))

    def step(j, carry, masked):
        start = pl.multiple_of(j * tq, tq)
        out = []
        scores = [jnp.dot(k_ref[pl.ds(start, tq), g * LANES:(g + 1) * LANES], qq[g],
                          preferred_element_type=F32) for g in range(heads)]
        for g in range(heads):
            m, l, acc = carry[g]
            vj = vt_ref[g * LANES:(g + 1) * LANES, pl.ds(start, tq)]
            s = scores[g]
            if masked:
                kpos = lax.broadcasted_iota(I32, s.shape, 0)
                qpos = lax.broadcasted_iota(I32, s.shape, 1) & (tq - 1)
                s = jnp.where(kpos <= qpos, s, -jnp.inf)
            m_new = jnp.maximum(m, jnp.max(s, axis=0, keepdims=True))
            alpha = jnp.exp2(m - m_new)
            p = jnp.exp2(s - m_new)
            l = l * alpha + jnp.sum(p, axis=0, keepdims=True)
            acc = acc * alpha + jnp.dot(vj, p.astype(BF16), preferred_element_type=F32)
            out.append((m_new, l, acc))
        return tuple(out)

    init = tuple((jnp.full((1, 2 * tq), -jnp.inf, F32), jnp.zeros((1, 2 * tq), F32),
                  jnp.zeros((ATT_DV, 2 * tq), F32)) for _ in range(heads))
    carry = lax.fori_loop(0, qi, lambda j, c: step(j, c, False), init)
    carry = step(qi, carry, True)
    for g in range(heads):
        _, l, acc = carry[g]
        o = acc / l
        o_ref[:, g * LANES:(g + 1) * LANES] = (o[:, :tq] - lam * o[:, tq:]).T


def _attn_prompt(lams, qt, kb, vt, *, tq, heads, lam_init, interpret=False):
    b, s, _ = kb.shape
    assert tq & (tq - 1) == 0 and ATT_HEADS % heads == 0
    lam_spec = pl.BlockSpec((1, ATT_DH), lambda bi, h, i: (0, 0))
    w = heads * LANES
    return pl.pallas_call(
        functools.partial(_attn_prompt_body, tq=tq, lam_init=lam_init),
        out_shape=jax.ShapeDtypeStruct((b, s, D_ATT), F32),
        grid=(b, ATT_HEADS // heads, s // tq),
        in_specs=[lam_spec] * 4 + [pl.BlockSpec((None, w, tq), lambda bi, h, i: (bi, h, i)),
                                   pl.BlockSpec((None, s, w), lambda bi, h, i: (bi, 0, h)),
                                   pl.BlockSpec((None, w, s), lambda bi, h, i: (bi, h, 0))],
        out_specs=pl.BlockSpec((None, tq, w), lambda bi, h, i: (bi, i, h)),
        compiler_params=_params(("parallel", "parallel", "arbitrary")),
        name="attn_prompt", interpret=interpret,
    )(*lams, qt, kb, vt)


def _attn_sample_body(pt_ref, lq1, lk1, lq2, lk2, q_ref, kn_ref, vn_ref, *rest, pp, t, lam_init):
    k_refs = rest[:pp]
    v_refs = rest[pp:2 * pp]
    o_ref = rest[2 * pp]
    qm_ref, m_ref, l_ref, acc_ref = rest[2 * pp + 1:]
    g = pl.program_id(1)
    rows = ATT_HEADS * 2 * t
    head_shift = (2 * t).bit_length() - 1

    @pl.when(g == 0)
    def _():
        q = q_ref[...] * (ATT_DH ** -0.5)
        lane = lax.broadcasted_iota(I32, (t, LANES), 1)
        blocks = []
        for h in range(ATT_HEADS):
            qh = q[:, h * LANES:(h + 1) * LANES]
            blocks.append(jnp.where(lane < ATT_DH, qh, 0.0))
            blocks.append(jnp.where(lane >= ATT_DH, qh, 0.0))
        qm_ref[...] = jnp.concatenate(blocks, axis=0).astype(BF16)
        m_ref[...] = jnp.full(m_ref.shape, -jnp.inf, F32)
        l_ref[...] = jnp.zeros(l_ref.shape, F32)
        acc_ref[...] = jnp.zeros(acc_ref.shape, F32)

    def online(s_list, v_list):
        m_prev = m_ref[...]
        m_new = m_prev
        for s in s_list:
            m_new = jnp.maximum(m_new, jnp.max(s, axis=-1, keepdims=True))
        alpha = jnp.exp(m_prev - m_new)
        l = l_ref[...] * alpha
        acc = acc_ref[...] * alpha
        for s, v in zip(s_list, v_list):
            p = jnp.exp(s - m_new)
            l = l + jnp.sum(p, axis=-1, keepdims=True)
            acc = acc + jnp.dot(p.astype(BF16), v, preferred_element_type=F32)
        l_ref[...] = l
        acc_ref[...] = acc
        m_ref[...] = m_new

    qm = qm_ref[...]
    nrow = k_refs[0].shape[0]
    same_head = ((lax.broadcasted_iota(I32, (rows, nrow), 0) >> head_shift)
                 == (lax.broadcasted_iota(I32, (rows, nrow), 1) & (ATT_HEADS - 1)))
    online([jnp.where(same_head,
                      lax.dot_general(qm, k_refs[j][...].astype(BF16), _NT, preferred_element_type=F32),
                      -jnp.inf) for j in range(pp)],
           [v_refs[j][...].astype(BF16) for j in range(pp)])

    @pl.when(g == pl.num_programs(1) - 1)
    def _():
        s2 = lax.dot_general(qm, kn_ref[...].astype(BF16), _NT, preferred_element_type=F32)
        r = lax.broadcasted_iota(I32, s2.shape, 0)
        c = lax.broadcasted_iota(I32, s2.shape, 1)
        ok = ((r >> head_shift) == (c & (ATT_HEADS - 1))) & ((c >> 2) <= (r & (t - 1)))
        online([jnp.where(ok, s2, -jnp.inf)], [vn_ref[...].astype(BF16)])
        lam = _lam(lq1[...], lk1[...], lq2[...], lk2[...], lam_init)
        o = acc_ref[...] / l_ref[...]
        for h in range(ATT_HEADS):
            oh = o[h * 2 * t:(h + 1) * 2 * t]
            o_ref[:, h * LANES:(h + 1) * LANES] = oh[:t] - lam * oh[t:]


def _attn_sample(page_table, lams, q, k_new, v_new, cache_k, cache_v, li, *, pp, lam_init, interpret=False):
    bd, t, _ = q.shape
    n_pages = page_table.shape[1]
    prow = cache_k.shape[2]
    assert n_pages % pp == 0 and t & (t - 1) == 0 and ATT_HEADS == 4
    rows = ATT_HEADS * 2 * t
    lam_spec = pl.BlockSpec((1, ATT_DH), lambda b, g, pt: (0, 0))
    tok_spec = pl.BlockSpec((None, t, D_ATT), lambda b, g, pt: (b, 0, 0))
    new_spec = pl.BlockSpec((None, t * ATT_HEADS, LANES), lambda b, g, pt: (b, 0, 0))

    def page_spec(j):
        return pl.BlockSpec((None, None, prow, LANES), lambda b, g, pt: (li, pt[b, g * pp + j], 0, 0))

    grid_spec = pltpu.PrefetchScalarGridSpec(
        num_scalar_prefetch=1,
        grid=(bd, n_pages // pp),
        in_specs=[lam_spec] * 4 + [tok_spec, new_spec, new_spec] + [page_spec(j) for j in range(pp)] * 2,
        out_specs=tok_spec,
        scratch_shapes=[pltpu.VMEM((rows, LANES), BF16), pltpu.VMEM((rows, 1), F32),
                        pltpu.VMEM((rows, 1), F32), pltpu.VMEM((rows, LANES), F32)],
    )
    return pl.pallas_call(
        functools.partial(_attn_sample_body, pp=pp, t=t, lam_init=lam_init),
        out_shape=jax.ShapeDtypeStruct((bd, t, D_ATT), F32),
        grid_spec=grid_spec,
        compiler_params=_params(("parallel", "arbitrary")),
        name="attn_sample", interpret=interpret,
    )(page_table, *lams, q, k_new, v_new, *([cache_k] * pp), *([cache_v] * pp))


def _mixer_out_body(x_ref, att_ref, u_ref, hist_ref, gsub_ref, cw_ref, cb_ref, lng_ref, lnb_ref,
                    wout_ref, o_ref, ucat_ref, ush_ref, conv_ref, *, tb, t, zero_first_hist, att_gain):
    hist = hist_ref[...]
    if zero_first_hist:
        hist = jnp.where(pl.program_id(1) == 0, 0.0, hist)
    ucat_ref[:, 0:CONV_HIST, :] = hist
    ucat_ref[:, CONV_HIST:CONV_HIST + t, :] = u_ref[...]
    rc = min(t, 64)
    off = CONV_HIST - (CONV_W - 1)
    span = t + CONV_HIST - 8
    for r in range(1, 8):
        ush_ref[r - 1, :, 0:span, :] = ucat_ref[:, r:r + span, :]
    for b in range(tb):
        for r0 in range(0, t, rc):
            acc = jnp.broadcast_to(cb_ref[...], (rc, D_CONV))
            for j in range(CONV_W):
                r = (j + off) % 8
                base = r0 + j + off - r
                src = ucat_ref[b, base:base + rc, :] if r == 0 else ush_ref[r - 1, b, base:base + rc, :]
                acc = acc + cw_ref[j:j + 1, :] * src
            conv_ref[b * t + r0:b * t + r0 + rc, :] = acc
    c = conv_ref[...]
    mu = jnp.mean(c, axis=-1, keepdims=True)
    cc = c - mu
    var = jnp.mean(cc * cc, axis=-1, keepdims=True)
    cn = cc * lax.rsqrt(var + EPS) * lng_ref[...] + lnb_ref[...]
    cact = cn * jax.nn.sigmoid(cn)
    m = tb * t
    att = att_ref[...].reshape(m, D_ATT)
    gsub = gsub_ref[...] * att_gain
    a = jnp.concatenate(
        [_rmsnorm(att[:, h * LANES:(h + 1) * LANES], gsub) for h in range(ATT_HEADS)], axis=1)
    y = jnp.dot(a.astype(BF16), wout_ref[0:D_ATT, :], preferred_element_type=F32)
    y = y + jnp.dot(cact.astype(BF16), wout_ref[D_ATT:, :], preferred_element_type=F32)
    o_ref[...] = (y + x_ref[...].reshape(m, -1)).reshape(o_ref.shape)


def _mixer_out(x, att, u, hist, weights, *, tb, t, hist_from_u, att_gain, interpret=False):
    b, tt, d = x.shape
    gsub, cw, cb, lng, lnb, wout = weights
    nb = t // CONV_HIST
    if hist_from_u:
        hist_spec = pl.BlockSpec((tb, CONV_HIST, D_CONV),
                                 lambda bi, i: (bi, jnp.maximum(i * nb - 1, 0), 0))
    else:
        hist_spec = pl.BlockSpec((tb, CONV_HIST, D_CONV), lambda bi, i: (bi, 0, 0))
    row = lambda w: pl.BlockSpec((1, w), lambda bi, i: (0, 0))
    tile = lambda w: pl.BlockSpec((tb, t, w), lambda bi, i: (bi, i, 0))
    return pl.pallas_call(
        functools.partial(_mixer_out_body, tb=tb, t=t, zero_first_hist=hist_from_u, att_gain=att_gain),
        out_shape=jax.ShapeDtypeStruct(x.shape, F32),
        grid=(b // tb, tt // t),
        in_specs=[tile(d), tile(D_ATT), tile(D_CONV), hist_spec, row(LANES),
                  pl.BlockSpec((CONV_W, D_CONV), lambda bi, i: (0, 0)),
                  row(D_CONV), row(D_CONV), row(D_CONV),
                  pl.BlockSpec((d, d), lambda bi, i: (0, 0))],
        out_specs=tile(d),
        scratch_shapes=[pltpu.VMEM((tb, CONV_HIST + t, D_CONV), F32),
                        pltpu.VMEM((7, tb, CONV_HIST + t - 8, D_CONV), F32),
                        pltpu.VMEM((tb * t, D_CONV), F32)],
        compiler_params=_params(("parallel", "arbitrary")),
        name="mixer_out", interpret=interpret,
    )(x, att, u, hist, gsub, cw, cb, lng, lnb, wout)


def _mem_kv_body(m_ref, g_ref, wk_ref, wv_ref, k_ref, v_ref):
    mn = _rmsnorm(m_ref[...], g_ref[...]).astype(BF16)
    k_ref[...] = jnp.dot(mn, wk_ref[...], preferred_element_type=F32)
    v_ref[...] = jnp.dot(mn, wv_ref[...], preferred_element_type=F32)


def _mem_kv(mem2d, g, wk, wv, *, tm, interpret=False):
    n, d = mem2d.shape
    tile = pl.BlockSpec((tm, d), lambda i: (i, 0))
    full = pl.BlockSpec((d, d), lambda i: (0, 0))
    return pl.pallas_call(
        _mem_kv_body,
        out_shape=[jax.ShapeDtypeStruct((n, d), F32)] * 2,
        grid=(n // tm,),
        in_specs=[tile, pl.BlockSpec((1, d), lambda i: (0, 0)), full, full],
        out_specs=[tile, tile],
        compiler_params=_params(("parallel",)),
        name="mem_kv", interpret=interpret,
    )(mem2d, g, wk, wv)


def _mem_attn_body(x_ref, g_ref, wq_ref, wo_ref, *rest, tb, t, per_head):
    mk_ref, mv_ref, o_ref = rest[0], rest[1], rest[-1]
    m = tb * t
    d = x_ref.shape[-1]
    dh = d // MEM_HEADS
    x = x_ref[...].reshape(m, d)
    hn = _rmsnorm(x, g_ref[...]).astype(BF16)
    q = (jnp.dot(hn, wq_ref[...], preferred_element_type=F32) * (dh ** -0.5)).astype(BF16)
    outs = []
    for b in range(tb):
        heads = []
        for h in range(MEM_HEADS):
            qh = q[b * t:(b + 1) * t, h * dh:(h + 1) * dh]
            if per_head:
                kh = mk_ref[b, :, h, :]
                vh = mv_ref[b, :, h, :]
                qh = qh.astype(F32)
            else:
                kh = mk_ref[b, :, h * dh:(h + 1) * dh].astype(BF16)
                vh = mv_ref[b, :, h * dh:(h + 1) * dh].astype(BF16)
            s = lax.dot_general(qh, kh, _NT, preferred_element_type=F32)
            e = jnp.exp(s - jnp.max(s, axis=-1, keepdims=True))
            p = e / jnp.sum(e, axis=-1, keepdims=True)
            heads.append(jnp.dot(p.astype(vh.dtype), vh, preferred_element_type=F32))
        outs.append(jnp.concatenate(heads, axis=1))
    o = jnp.concatenate(outs, axis=0) if tb > 1 else outs[0]
    y = jnp.dot(o.astype(BF16), wo_ref[...], preferred_element_type=F32) + x
    o_ref[...] = y


def _mem_attn(x, mk, mv, g, wq, wo, *, tb, t, n_rows, row0=0, into=None, interpret=False):
    b, tt, d = x.shape
    mt = mk.shape[1]
    m = tb * t
    nt = tt // t
    assert row0 % m == 0
    tile = pl.BlockSpec((tb, t, d), lambda bi, i: (bi, i, 0))
    per_head = mk.ndim == 4
    if per_head:
        mem = [pl.BlockSpec((tb, mt) + mk.shape[2:], lambda bi, i: (bi, 0, 0, 0))]
    else:
        mem = [pl.BlockSpec((tb, mt, d), lambda bi, i: (bi, 0, 0))]
    full = pl.BlockSpec((d, d), lambda bi, i: (0, 0))
    in_specs = [tile, pl.BlockSpec((1, d), lambda bi, i: (0, 0)), full, full] + mem + mem
    args = [x, g, wq, wo] + [mk] * len(mem) + [mv] * len(mem)
    aliases = {}
    if into is not None:
        in_specs.append(pl.BlockSpec(memory_space=pl.ANY))
        args.append(into)
        aliases = {len(args) - 1: 0}
    return pl.pallas_call(
        functools.partial(_mem_attn_body, tb=tb, t=t, per_head=per_head),
        out_shape=jax.ShapeDtypeStruct((n_rows, d), F32),
        grid=(b // tb, nt),
        in_specs=in_specs,
        out_specs=pl.BlockSpec((m, d), lambda bi, i: (row0 // m + bi * nt + i, 0)),
        input_output_aliases=aliases,
        compiler_params=_params(("parallel", "arbitrary")),
        name="mem_attn", interpret=interpret,
    )(*args)


def _topk16(s, want_rank):
    r = s.shape[0]
    iota = lax.broadcasted_iota(I32, s.shape, 0)
    vals, idxs = [], []
    rank = jnp.full(s.shape, PEER_TOPK, I32) if want_rank else None
    for k in range(PEER_TOPK):
        m = jnp.max(s, axis=0, keepdims=True)
        idx = jnp.min(jnp.where(s == m, iota, r), axis=0, keepdims=True)
        hit = iota == idx
        if want_rank:
            rank = jnp.where(hit, k, rank)
        s = jnp.where(hit, -jnp.inf, s)
        vals.append(m)
        idxs.append(idx)
    return jnp.concatenate(vals, axis=0), jnp.concatenate(idxs, axis=0), rank


def _route_exact(s1, s2):
    sv1, _, r1 = _topk16(s1, True)
    sv2, _, r2 = _topk16(s2, True)
    cand = jnp.concatenate([sv1[k:k + 1] + sv2 for k in range(PEER_TOPK)], axis=0)
    tv, tp, _ = _topk16(cand, False)
    k1 = tp >> 4
    n1 = jnp.zeros(s1.shape, F32)
    for k in range(PEER_TOPK):
        n1 = n1 + jnp.where(r1 == k1[k:k + 1], 1.0, 0.0)
    z = jnp.sum(jnp.exp(tv - tv[0:1]), axis=0, keepdims=True)
    return n1, r2, z


def _topk16_distinct(s, want_rank):
    vals = []
    rank = jnp.full(s.shape, PEER_TOPK, I32) if want_rank else None
    for k in range(PEER_TOPK):
        m = jnp.max(s, axis=0, keepdims=True)
        hit = s == m
        if want_rank:
            rank = jnp.where(hit, k, rank)
        s = jnp.where(hit, -jnp.inf, s)
        vals.append(m)
    return vals, rank, s


_CAND_GROUPS = (
    ((0, 0, 0, 8),),
    ((0, 0, 8, 8),),
    ((1, 0, 0, 8),),
    ((2, 0, 0, 5), (4, 5, 0, 3)),
    ((3, 0, 0, 4), (5, 4, 0, 2), (6, 6, 0, 2)),
    ((7, 0, 0, 2),) + tuple((8 + i, 2 + i, 0, 1) for i in range(6)),
    ((14, 0, 0, 1), (15, 1, 0, 1)),
)


def _route_distinct(s1, s2):
    t = s1.shape[1]
    sv1, r1, _ = _topk16_distinct(s1, True)
    sv2, r2, _ = _topk16_distinct(s2, True)
    sv2_lo = jnp.concatenate(sv2[:8], axis=0)
    sv2_hi = jnp.concatenate(sv2[8:], axis=0)
    row = lax.broadcasted_iota(I32, (8, t), 0)
    groups = []
    for pieces in _CAND_GROUPS:
        val = jnp.full((8, t), -jnp.inf, F32)
        for k1, off, k2, nk in pieces:
            src = sv2_hi if k2 else sv2_lo
            if off:
                src = pltpu.roll(src, off, 0)
            val = jnp.where((row >= off) & (row < off + nk), sv1[k1] + src, val)
        groups.append(val)
    cand = jnp.concatenate(groups, axis=0)
    tv, _, left = _topk16_distinct(cand, False)
    won = jnp.where(left != cand, 1.0, 0.0)
    cnt = [jnp.zeros((1, t), F32) for _ in range(PEER_TOPK)]
    for gi, pieces in enumerate(_CAND_GROUPS):
        w = won[gi * 8:(gi + 1) * 8]
        for k1, off, _, nk in pieces:
            cnt[k1] = cnt[k1] + jnp.sum(jnp.where((row >= off) & (row < off + nk), w, 0.0),
                                        axis=0, keepdims=True)
    n1 = jnp.zeros(s1.shape, F32)
    for k in range(PEER_TOPK):
        n1 = jnp.where(r1 == k, cnt[k], n1)
    z = jnp.zeros((1, t), F32)
    for k in range(PEER_TOPK):
        z = z + jnp.exp(tv[k] - tv[0])
    picked = lambda r: jnp.sum(jnp.where(r < PEER_TOPK, 1.0, 0.0), axis=0, keepdims=True)
    total = cnt[0]
    for k in range(1, PEER_TOPK):
        total = total + cnt[k]
    tied = (picked(r1) != PEER_TOPK) | (picked(r2) != PEER_TOPK) | (total != PEER_TOPK)
    return n1, r2, z, sv1[0], sv2[0], tied


def _peer_route_body(x_ref, g_ref, wpq_ref, sk_ref, xn_ref, n1_ref, e1_ref, r2_ref, e2_ref, q_ref):
    xnt = _rmsnorm(x_ref[...], g_ref[...]).T.astype(BF16)
    xn_ref[...] = xnt
    q_ref[...] = jnp.dot(wpq_ref[...], xnt, preferred_element_type=F32).astype(BF16)

    group = 4

    def heads(hg, carry):
        def scores(hc):
            r0 = pl.multiple_of(hc * LANES, LANES)
            return jnp.dot(sk_ref[hc], q_ref[pl.ds(r0, LANES), :], preferred_element_type=F32)

        done = []
        any_tied = None
        for hh in range(group):
            h = hg * group + hh
            s1 = scores(h * 2)
            s2 = scores(h * 2 + 1)
            n1, r2, z, top1, top2, tied = _route_distinct(s1, s2)
            ex2 = jnp.exp(s2 - top2)
            n1_ref[h] = n1
            e1_ref[h] = jnp.exp(s1 - top1)
            r2_ref[h] = r2.astype(BF16)
            e2_ref[h] = (ex2 / z).astype(BF16)
            done.append((h, s1, s2, ex2))
            any_tied = tied if any_tied is None else any_tied | tied

        @pl.when(jnp.max(jnp.where(any_tied, 1.0, 0.0)) > 0.5)
        def _():
            for h, s1, s2, ex2 in done:
                n1x, r2x, zx = _route_exact(s1, s2)
                n1_ref[h] = n1x
                r2_ref[h] = r2x.astype(BF16)
                e2_ref[h] = (ex2 / zx).astype(BF16)

        return carry

    lax.fori_loop(0, PEER_HEADS // group, heads, 0)


def _peer_route(x2d, g, wpq, sk, *, tm, interpret=False):
    n, d = x2d.shape
    dq = wpq.shape[0]
    route = pl.BlockSpec((PEER_HEADS, N_KEYS, tm), lambda i: (0, 0, i))
    route_f32 = jax.ShapeDtypeStruct((PEER_HEADS, N_KEYS, n), F32)
    route_b16 = jax.ShapeDtypeStruct((PEER_HEADS, N_KEYS, n), BF16)
    return pl.pallas_call(
        _peer_route_body,
        out_shape=[jax.ShapeDtypeStruct((d, n), BF16), route_f32, route_f32, route_b16, route_b16],
        grid=(n // tm,),
        in_specs=[pl.BlockSpec((tm, d), lambda i: (i, 0)),
                  pl.BlockSpec((1, d), lambda i: (0, 0)),
                  pl.BlockSpec((dq, d), lambda i: (0, 0)),
                  pl.BlockSpec(sk.shape, lambda i: (0, 0, 0))],
        out_specs=[pl.BlockSpec((d, tm), lambda i: (0, i))] + [route] * 4,
        scratch_shapes=[pltpu.VMEM((dq, tm), BF16)],
        compiler_params=_params(("parallel",)),
        name="peer_route", interpret=interpret,
    )(x2d, g, wpq, sk)


def _row_bf16(row):
    slab = jnp.broadcast_to(row, (16, row.shape[1])).astype(BF16)
    return jnp.concatenate([slab] * (N_KEYS // 16), axis=0)


def _peer_dense_body(x_ref, xn_ref, n1_ref, e1_ref, r2_ref, e2_ref, u_ref, vt_ref, gf_ref, y_ref, y2_ref,
                     acc_ref, w_ref, *, ti, tiles_first):
    j = pl.program_id(1)

    @pl.when(j == 0)
    def _():
        acc_ref[...] = jnp.zeros(acc_ref.shape, F32)

    xn = xn_ref[...]
    ht = [jnp.dot(u_ref[il * N_KEYS:(il + 1) * N_KEYS, :], xn, preferred_element_type=F32)
          for il in range(ti)]
    out = None
    kc = 2 * N_KEYS
    assert ti % 8 == 0
    for il in range(ti):
        slab = pl.ds(pl.multiple_of(j * ti + (il // 8) * 8, 8), 8)
        sub = il % 8
        rows = slice(il * N_KEYS, (il + 1) * N_KEYS)
        gate = None
        for h in range(PEER_HEADS):
            n1 = _row_bf16(n1_ref[h, slab, :][sub:sub + 1])
            e1 = _row_bf16(e1_ref[h, slab, :][sub:sub + 1])
            term = jnp.where(r2_ref[h] < n1, e2_ref[h], jnp.zeros((), BF16)) * e1
            gate = term if gate is None else gate + term
        w_ref[rows, :] = _gelu(ht[il]).astype(BF16) * gate
        if (il + 1) * N_KEYS % kc == 0:
            k0 = (il + 1) * N_KEYS - kc
            part = jnp.dot(vt_ref[:, k0:k0 + kc], w_ref[k0:k0 + kc, :], preferred_element_type=F32)
            out = part if out is None else out + part
    acc_ref[...] += out

    last = j == pl.num_programs(1) - 1
    first_group = pl.program_id(0) < tiles_first

    @pl.when(last & first_group)
    def _():
        y_ref[...] = _rmsnorm(x_ref[...] + acc_ref[...].T, gf_ref[...])

    @pl.when(last & jnp.logical_not(first_group))
    def _():
        y2_ref[...] = _rmsnorm(x_ref[...] + acc_ref[...].T, gf_ref[...])


def _peer_dense(x2d, xn, n1, e1, r2, e2, u_bf16, vt_bf16, g_final, *, n_first, tm, ti, interpret=False):
    n, d = x2d.shape
    ne = u_bf16.shape[0]
    te = ti * N_KEYS
    assert n_first % tm == 0 and (n - n_first) % tm == 0 and n > n_first
    tf = n_first // tm
    tok = pl.BlockSpec((tm, d), lambda i, j: (i, 0))
    route = pl.BlockSpec((PEER_HEADS, N_KEYS, tm), lambda i, j: (0, 0, i))
    return pl.pallas_call(
        functools.partial(_peer_dense_body, ti=ti, tiles_first=tf),
        out_shape=[jax.ShapeDtypeStruct((n_first, d), F32), jax.ShapeDtypeStruct((n - n_first, d), F32)],
        grid=(n // tm, ne // te),
        in_specs=[tok, pl.BlockSpec((d, tm), lambda i, j: (0, i)), route, route, route, route,
                  pl.BlockSpec((te, d), lambda i, j: (j, 0)),
                  pl.BlockSpec((d, te), lambda i, j: (0, j)),
                  pl.BlockSpec((1, d), lambda i, j: (0, 0))],
        out_specs=[pl.BlockSpec((tm, d), lambda i, j: (jnp.minimum(i, tf - 1), 0)),
                   pl.BlockSpec((tm, d), lambda i, j: (jnp.maximum(i - tf, 0), 0))],
        scratch_shapes=[pltpu.VMEM((d, tm), F32), pltpu.VMEM((te, tm), BF16)],
        compiler_params=_params(("parallel", "arbitrary")),
        name="peer_dense", interpret=interpret,
    )(x2d, xn, n1, e1, r2, e2, u_bf16, vt_bf16, g_final)


def _pick(n, pref):
    t = min(n, pref)
    while n % t:
        t //= 2
    return t


def _forward(x_prompt, x_sample, cache_k, cache_v, state_conv, cache_mem_k, cache_mem_v, page_table,
             mem_prompt, g_norm_mix, w_in, lam_q1, lam_k1, lam_q2, lam_k2, g_subln, conv_w, conv_b,
             conv_ln_g, conv_ln_b, w_out, g_norm_mem, g_mem_src, w_mq, w_mk, w_mv, w_mo, g_norm_peer,
             w_pq, sub_keys, u_tab, v_tab, g_final, *, interpret=False):
    depth = w_in.shape[0]
    assert depth == 1, "single layer: the final norm is fused into the PEER kernel"
    li = 0
    b, s, d = x_prompt.shape
    bd, ts, _ = x_sample.shape
    past = page_table.shape[1] * cache_k.shape[2]
    n_pool, page = cache_k.shape[1], cache_k.shape[2]
    mt = mem_prompt.shape[1]
    kw = dict(interpret=interpret)
    row = lambda a: a.reshape(1, -1)

    lam_init = 0.8 - 0.6 * math.exp(-0.3 * li)
    lams = (row(lam_q1[li]), row(lam_k1[li]), row(lam_q2[li]), row(lam_k2[li]))
    w_in_b = w_in[li].astype(BF16)
    mix_w = (row(g_subln[li]), conv_w[li], row(conv_b[li]), row(conv_ln_g[li]), row(conv_ln_b[li]),
             w_out[li].astype(BF16))

    tm_p = _pick(s, 512)
    w_nat = w_in_b[:, D_ATT:]
    w_t = jnp.concatenate([w_in_b[:, :D_ATT], w_in_b[:, 2 * D_ATT:3 * D_ATT]], axis=1).T
    qtp, kp, kbp, vp, vtp, up = _mixer_in_t(x_prompt.reshape(b * s, d), row(g_norm_mix[li]), w_nat, w_t,
                                            _rope_tables(jnp.arange(s, dtype=I32)), b=b, tm=tm_p, **kw)
    ns = bd * ts
    tm_s = _pick(ns, 512)
    cos_s, sin_s, _, _ = _rope_tables(past + jnp.arange(ts, dtype=I32))
    cos_s = jnp.tile(cos_s, (tm_s // ts, 1))
    sin_s = jnp.tile(sin_s, (tm_s // ts, 1))
    qs, ks, vs, us = _mixer_in(x_sample.reshape(ns, d), row(g_norm_mix[li]), w_in_b, cos_s, sin_s,
                               tm=tm_s, **kw)
    r3 = lambda a, bb: a.reshape(bb, -1, a.shape[-1])
    kp, kbp, vp, up = (r3(a, b) for a in (kp, kbp, vp, up))
    qs, ks, vs, us = (r3(a, bd) for a in (qs, ks, vs, us))

    att_p = _attn_prompt(lams, qtp, kbp, vtp, tq=_pick(s, 512), heads=4, lam_init=lam_init, **kw)
    flat = lambda c: c.reshape(c.shape[0], n_pool, page * ATT_HEADS, LANES)
    att_s = _attn_sample(page_table, lams, qs, ks.reshape(bd, ts * ATT_HEADS, LANES),
                         vs.reshape(bd, ts * ATT_HEADS, LANES), flat(cache_k), flat(cache_v), li,
                         pp=_pick(page_table.shape[1], 32), lam_init=lam_init, **kw)

    xp = _mixer_out(x_prompt, att_p, up, up, mix_w, tb=1, t=tm_p, hist_from_u=True,
                    att_gain=1.0 - lam_init, **kw)
    state = state_conv[li]
    hist_s = jnp.pad(state, ((0, 0), (CONV_HIST - state.shape[1], 0), (0, 0)))
    xs = _mixer_out(x_sample, att_s, us, hist_s, mix_w, tb=_pick(bd, 16), t=ts, hist_from_u=False,
                    att_gain=1.0 - lam_init, **kw)
    conv_p = up[:, s - (CONV_W - 1):]
    conv_s = jnp.concatenate([state, us], axis=1)[:, -(CONV_W - 1):]

    mk, mv = _mem_kv(mem_prompt.reshape(b * mt, d), row(g_mem_src[li]), w_mk[li].astype(BF16),
                     w_mv[li].astype(BF16), tm=_pick(b * mt, 512), **kw)
    mk = mk.reshape(b, mt, d)
    mv = mv.reshape(b, mt, d)
    wq_b, wo_b = w_mq[li].astype(BF16), w_mo[li].astype(BF16)
    n_all = b * s + ns
    x_all = _mem_attn(xp, mk, mv, row(g_norm_mem[li]), wq_b, wo_b, tb=1, t=tm_p, n_rows=n_all, **kw)
    x_all = _mem_attn(xs, cache_mem_k[li], cache_mem_v[li], row(g_norm_mem[li]), wq_b, wo_b,
                      tb=_pick(bd, 8), t=ts, n_rows=n_all, row0=b * s, into=x_all, **kw)

    sk = sub_keys[li].reshape(PEER_HEADS * 2, N_KEYS, -1).astype(BF16)
    xn, n1, e1, r2, e2 = _peer_route(x_all, row(g_norm_peer[li]), w_pq[li].T.astype(BF16), sk,
                                     tm=_pick(n_all, 256), **kw)
    y_prompt, y_sample = _peer_dense(x_all, xn, n1, e1, r2, e2, u_tab[li].astype(BF16),
                                     v_tab[li].T.astype(BF16), row(g_final), n_first=b * s,
                                     tm=_pick(math.gcd(b * s, ns), 512), ti=_pick(N_KEYS, 16), **kw)
    y_prompt = y_prompt.reshape(b, s, d)
    y_sample = y_sample.reshape(bd, ts, d)

    k4 = lambda a: a.reshape(1, a.shape[0], a.shape[1], ATT_HEADS, -1)
    return (y_prompt, y_sample, k4(kp), k4(vp), conv_p[None],
            mk.reshape(1, b, mt, MEM_HEADS, -1), mv.reshape(1, b, mt, MEM_HEADS, -1),
            k4(ks), k4(vs), conv_s[None])


def kernel(x_prompt, x_sample, cache_k, cache_v, state_conv, cache_mem_k, cache_mem_v, page_table, mem_prompt, g_norm_mix, w_in, lam_q1, lam_k1, lam_q2, lam_k2, g_subln, conv_w, conv_b, conv_ln_g, conv_ln_b, w_out, g_norm_mem, g_mem_src, w_mq, w_mk, w_mv, w_mo, g_norm_peer, w_pq, sub_keys, u_tab, v_tab, g_final):
    return _forward(x_prompt, x_sample, cache_k, cache_v, state_conv, cache_mem_k, cache_mem_v, page_table,
                    mem_prompt, g_norm_mix, w_in, lam_q1, lam_k1, lam_q2, lam_k2, g_subln, conv_w, conv_b,
                    conv_ln_g, conv_ln_b, w_out, g_norm_mem, g_mem_src, w_mq, w_mk, w_mv, w_mo, g_norm_peer,
                    w_pq, sub_keys, u_tab, v_tab, g_final)
```
